```python
import math
import jax, jax.numpy as jnp
from jax import lax
import numpy as np

D_MODEL = 1024
BATCH = 1
SEQ = 16384
DEPTH = 1
DEC_BATCH = 128
DEC_SEQ = 8
PAST_LEN = 8192
PAGE_SIZE = 128

MIX_WIDTH = D_MODEL
ATTN_WIDTH = MIX_WIDTH // 2
POOL_WIDTH = MIX_WIDTH - ATTN_WIDTH
N_HEADS = 4
V_DIM = ATTN_WIDTH // N_HEADS
QK_DIM = V_DIM // 2
QK_SCALE = QK_DIM ** -0.5
Q_W = N_HEADS * 2 * QK_DIM
K_W = N_HEADS * 2 * QK_DIM
V_W = N_HEADS * V_DIM
IN_WIDTH = Q_W + K_W + V_W + POOL_WIDTH
POOL_WINDOWS = (2, 4, 8, 16)
N_POOL_GROUPS = len(POOL_WINDOWS)
POOL_GROUP = POOL_WIDTH // N_POOL_GROUPS
POOL_STATE = max(POOL_WINDOWS) - 1
NUM_BUCKETS = 32
MAX_DISTANCE = 128
Q_BLOCK = 128
N_EXPERTS = 32
TOP_K = 4
D_FF = D_MODEL
SWIGLU_LIMIT = 7.0
SWIGLU_ALPHA = 1.702
MOE_BLOCK = 128
DN_ALPHA = (2.0 * DEPTH) ** 0.25
DN_BETA = (8.0 * DEPTH) ** -0.25
LN_EPS = 1e-5

kernel_name = 'hybrid_diffattn_pool_moe_step'


def layer_norm(x, g, b):
    xf = x.astype(jnp.float32)
    mu = jnp.mean(xf, axis=-1, keepdims=True)
    var = jnp.mean(jnp.square(xf - mu), axis=-1, keepdims=True)
    return ((xf - mu) * lax.rsqrt(var + LN_EPS) * g.astype(jnp.float32) + b.astype(jnp.float32)).astype(x.dtype)


def rms_norm(x, g):
    xf = x.astype(jnp.float32)
    return (xf * lax.rsqrt(jnp.mean(jnp.square(xf), axis=-1, keepdims=True) + LN_EPS) * g.astype(jnp.float32)).astype(x.dtype)


def t5_bucket(dist):
    max_exact = NUM_BUCKETS // 2
    df = jnp.maximum(dist, 1).astype(jnp.float32)
    large = max_exact + (jnp.log(df / max_exact) / math.log(MAX_DISTANCE / max_exact) * (NUM_BUCKETS - max_exact)).astype(jnp.int32)
    large = jnp.minimum(large, NUM_BUCKETS - 1)
    return jnp.where(dist < max_exact, dist, large)


def rel_bias(q_pos, k_pos, table):
    dist = jnp.maximum(q_pos[:, None] - k_pos[None, :], 0)
    return jnp.transpose(table[t5_bucket(dist)], (2, 0, 1)).astype(jnp.float32)


def diff_attend(q, k, v, bias, mask, lam):
    s = jnp.einsum('qhcd,khcd->hcqk', q, k, preferred_element_type=jnp.float32) * QK_SCALE + bias[:, None]
    s = jnp.where(mask, s, -jnp.inf)
    p = jax.nn.softmax(s, axis=-1)
    a = p[:, 0] - lam * p[:, 1]
    return jnp.einsum('hqk,khv->qhv', a.astype(v.dtype), v)


def attn_prompt(q, k, v, table, lam):
    B_, S = q.shape[0], q.shape[1]
    nb = S // Q_BLOCK
    k_pos = jnp.arange(S)
    qb = jnp.moveaxis(q.reshape(B_, nb, Q_BLOCK, N_HEADS, 2, QK_DIM), 1, 0)

    def block(args):
        q_blk, i = args
        q_pos = i * Q_BLOCK + jnp.arange(Q_BLOCK)
        bias = rel_bias(q_pos, k_pos, table)
        mask = q_pos[:, None] >= k_pos[None, :]
        return jax.vmap(lambda qq, kk, vv: diff_attend(qq, kk, vv, bias, mask, lam))(q_blk, k, v)

    o = lax.map(block, (qb, jnp.arange(nb)))
    return jnp.moveaxis(o, 0, 1).reshape(B_, S, N_HEADS, V_DIM)


def attn_sample(q, k_new, v_new, k_cache, v_cache, page_table, table, lam):
    past = page_table.shape[1] * PAGE_SIZE
    T = q.shape[1]
    q_pos = past + jnp.arange(T)
    k_pos = jnp.arange(past + T)
    bias = rel_bias(q_pos, k_pos, table)
    mask = q_pos[:, None] >= k_pos[None, :]

    def one(args):
        qq, kn, vn, pt = args
        kp = k_cache[pt].reshape(past, N_HEADS, 2, QK_DIM)
        vp = v_cache[pt].reshape(past, N_HEADS, V_DIM)
        kk = jnp.concatenate([kp, kn.astype(kp.dtype)], axis=0)
        vv = jnp.concatenate([vp, vn.astype(vp.dtype)], axis=0)
        return diff_attend(qq, kk, vv, bias, mask, lam)

    return lax.map(one, (q, k_new, v_new, page_table))


def pool_mix(u_ext, n_hist, start_pos, w_pool, pool_scale):
    B_, L, C = u_ext.shape
    T = L - n_hist
    uf = u_ext.astype(jnp.float32)
    cs = jnp.concatenate([jnp.zeros((B_, 1, C), jnp.float32), jnp.cumsum(uf, axis=1)], axis=1)
    r = n_hist + jnp.arange(T)
    pos = start_pos + jnp.arange(T)
    hi = cs[:, n_hist + 1:]
    means = []
    for g, w in enumerate(POOL_WINDOWS):
        sl = slice(g * POOL_GROUP, (g + 1) * POOL_GROUP)
        lo = cs[:, jnp.maximum(r + 1 - w, 0), sl]
        cnt = jnp.minimum(pos + 1, w).astype(jnp.float32)[None, :, None]
        means.append((hi[..., sl] - lo) / cnt)
    pooled = (jnp.concatenate(means, axis=-1) - uf[:, n_hist:]).astype(u_ext.dtype)
    z = jnp.einsum('btgc,gcd->btgd', pooled.reshape(B_, T, N_POOL_GROUPS, POOL_GROUP), w_pool)
    return z.reshape(B_, T, POOL_WIDTH) * pool_scale


def split_proj(h, w_in):
    B_, T = h.shape[0], h.shape[1]
    p = h @ w_in
    q = p[..., :Q_W].reshape(B_, T, N_HEADS, 2, QK_DIM)
    k = p[..., Q_W:Q_W + K_W].reshape(B_, T, N_HEADS, 2, QK_DIM)
    v = p[..., Q_W + K_W:Q_W + K_W + V_W].reshape(B_, T, N_HEADS, V_DIM)
    u = p[..., Q_W + K_W + V_W:]
    return q, k, v, u


def moe(x, w_router, b_router, w_gu, b_gu, w_down, b_down):
    shp = x.shape
    xt = x.reshape(-1, shp[-1])
    T = xt.shape[0]
    logits = xt.astype(jnp.float32) @ w_router.astype(jnp.float32) + b_router.astype(jnp.float32)
    top_v, top_e = lax.top_k(logits, TOP_K)
    gates = jax.nn.softmax(top_v, axis=-1)
    n_assign = T * TOP_K
    flat_e = top_e.reshape(-1)
    flat_tok = jnp.repeat(jnp.arange(T, dtype=jnp.int32), TOP_K)
    flat_g = gates.reshape(-1)
    order = jnp.argsort(flat_e)
    se = flat_e[order]
    counts = jnp.bincount(flat_e, length=N_EXPERTS)
    padded = (counts + MOE_BLOCK - 1) // MOE_BLOCK * MOE_BLOCK
    pad_end = jnp.cumsum(padded)
    pad_start = pad_end - padded
    start = jnp.cumsum(counts) - counts
    dest = pad_start[se] + jnp.arange(n_assign) - start[se]
    n_blocks = -(-n_assign // MOE_BLOCK) + N_EXPERTS
    rows = n_blocks * MOE_BLOCK
    row_tok = jnp.full((rows,), T, jnp.int32).at[dest].set(flat_tok[order])
    row_gate = jnp.zeros((rows,), jnp.float32).at[dest].set(flat_g[order])
    block_e = jnp.minimum(jnp.searchsorted(pad_end, jnp.arange(n_blocks) * MOE_BLOCK, side='right'), N_EXPERTS - 1)
    x_pad = jnp.concatenate([xt, jnp.zeros((1, xt.shape[1]), xt.dtype)], axis=0)

    def run(args):
        tok, g, e = args
        gu = x_pad[tok] @ w_gu[e] + b_gu[e]
        gate = jnp.minimum(gu[:, :D_FF], SWIGLU_LIMIT)
        up = jnp.clip(gu[:, D_FF:], -SWIGLU_LIMIT, SWIGLU_LIMIT)
        hdn = (up + 1.0) * (gate * jax.nn.sigmoid(SWIGLU_ALPHA * gate))
        return (hdn @ w_down[e] + b_down[e]) * g.astype(xt.dtype)[:, None]

    yb = lax.map(run, (row_tok.reshape(n_blocks, MOE_BLOCK), row_gate.reshape(n_blocks, MOE_BLOCK), block_e))
    y = jnp.zeros((T + 1, xt.shape[1]), xt.dtype).at[row_tok].add(yb.reshape(rows, -1).astype(xt.dtype))[:T]
    return y.reshape(shp)


def setup_inputs(seed: int = 0) -> dict:
    key = jax.random.key(seed)
    ks = jax.random.split(key, 32)
    f32 = jnp.float32
    n_pages = PAST_LEN // PAGE_SIZE
    used = DEC_BATCH * n_pages
    n_pool_pages = used + max(1, used // 4)
    page_table = jax.random.permutation(ks[0], n_pool_pages)[:used].reshape(DEC_BATCH, n_pages).astype(jnp.int32)
    col_scale = jnp.concatenate([jnp.ones((Q_W + K_W,), f32), jnp.full((V_W + POOL_WIDTH,), DN_BETA, f32)])
    nrm = lambda k, s, sc: jax.random.normal(k, s, f32) * sc
    return {
        'x_prompt': nrm(ks[1], (BATCH, SEQ, D_MODEL), 1.0),
        'x_sample': nrm(ks[2], (DEC_BATCH, DEC_SEQ, D_MODEL), 1.0),
        'cache_k': nrm(ks[3], (DEPTH, n_pool_pages, PAGE_SIZE, N_HEADS, 2, QK_DIM), 1.0),
        'cache_v': nrm(ks[4], (DEPTH, n_pool_pages, PAGE_SIZE, N_HEADS, V_DIM), DN_BETA),
        'state_pool': nrm(ks[5], (DEPTH, DEC_BATCH, POOL_STATE, POOL_WIDTH), DN_BETA),
        'page_table': page_table,
        'ln_in_g': 1.0 + nrm(ks[6], (D_MODEL,), 0.02),
        'ln_in_b': nrm(ks[7], (D_MODEL,), 0.02),
        'w_in': nrm(ks[8], (DEPTH, D_MODEL, IN_WIDTH), D_MODEL ** -0.5) * col_scale,
        'lambda_q1': nrm(ks[9], (DEPTH, QK_DIM), 0.1),
        'lambda_k1': nrm(ks[10], (DEPTH, QK_DIM), 0.1),
        'lambda_q2': nrm(ks[11], (DEPTH, QK_DIM), 0.1),
        'lambda_k2': nrm(ks[12], (DEPTH, QK_DIM), 0.1),
        'subln_g': 1.0 + nrm(ks[13], (DEPTH, V_DIM), 0.02),
        'rel_bias_table': nrm(ks[14], (NUM_BUCKETS, N_HEADS), 0.3),
        'w_pool': nrm(ks[15], (DEPTH, N_POOL_GROUPS, POOL_GROUP, POOL_GROUP), POOL_GROUP ** -0.5),
        'pool_scale': 1.0 + nrm(ks[16], (DEPTH, POOL_WIDTH), 0.02),
        'w_out': nrm(ks[17], (DEPTH, MIX_WIDTH, D_MODEL), MIX_WIDTH ** -0.5 * DN_BETA),
        'ln1_g': 1.0 + nrm(ks[18], (DEPTH, D_MODEL), 0.02),
        'ln1_b': nrm(ks[19], (DEPTH, D_MODEL), 0.02),
        'w_router': nrm(ks[20], (DEPTH, D_MODEL, N_EXPERTS), D_MODEL ** -0.5),
        'b_router': nrm(ks[21], (DEPTH, N_EXPERTS), 0.01),
        'w_gate_up': nrm(ks[22], (DEPTH, N_EXPERTS, D_MODEL, 2 * D_FF), D_MODEL ** -0.5 * DN_BETA),
        'b_gate_up': nrm(ks[23], (DEPTH, N_EXPERTS, 2 * D_FF), 0.01),
        'w_down': nrm(ks[24], (DEPTH, N_EXPERTS, D_FF, D_MODEL), D_FF ** -0.5 * DN_BETA),
        'b_down': nrm(ks[25], (DEPTH, N_EXPERTS, D_MODEL), 0.01),
        'ln2_g': 1.0 + nrm(ks[26], (DEPTH, D_MODEL), 0.02),
        'ln2_b': nrm(ks[27], (DEPTH, D_MODEL), 0.02),
    }


def reference(x_prompt, x_sample, cache_k, cache_v, state_pool, page_table,
              ln_in_g, ln_in_b, w_in, lambda_q1, lambda_k1, lambda_q2, lambda_k2, subln_g,
              rel_bias_table, w_pool, pool_scale, w_out, ln1_g, ln1_b,
              w_router, b_router, w_gate_up, b_gate_up, w_down, b_down, ln2_g, ln2_b):
    past = page_table.shape[1] * PAGE_SIZE
    hp = layer_norm(x_prompt, ln_in_g, ln_in_b)
    hs = layer_norm(x_sample, ln_in_g, ln_in_b)
    kp_l, vp_l, pp_l, ks_l, vs_l, ps_l = [], [], [], [], [], []
    for l in range(DEPTH):
        lambda_init = 0.8 - 0.6 * math.exp(-0.3 * l)
        lam = (jnp.exp(jnp.sum(lambda_q1[l].astype(jnp.float32) * lambda_k1[l].astype(jnp.float32)))
               - jnp.exp(jnp.sum(lambda_q2[l].astype(jnp.float32) * lambda_k2[l].astype(jnp.float32))) + lambda_init)
        q, k, v, u = split_proj(hp, w_in[l])
        a = attn_prompt(q, k, v, rel_bias_table, lam)
        a = (rms_norm(a, subln_g[l]) * (1.0 - lambda_init)).reshape(hp.shape[0], hp.shape[1], ATTN_WIDTH)
        z = pool_mix(u, 0, 0, w_pool[l], pool_scale[l])
        mix = jnp.concatenate([a.astype(z.dtype), z], axis=-1) @ w_out[l]
        hp_new = layer_norm(DN_ALPHA * hp + mix, ln1_g[l], ln1_b[l])
        kp_l.append(k)
        vp_l.append(v)
        pp_l.append(u[:, -POOL_STATE:])
        q2, k2, v2, u2 = split_proj(hs, w_in[l])
        a2 = attn_sample(q2, k2, v2, cache_k[l], cache_v[l], page_table, rel_bias_table, lam)
        a2 = (rms_norm(a2, subln_g[l]) * (1.0 - lambda_init)).reshape(hs.shape[0], hs.shape[1], ATTN_WIDTH)
        u2_ext = jnp.concatenate([state_pool[l].astype(u2.dtype), u2], axis=1)
        z2 = pool_mix(u2_ext, POOL_STATE, past, w_pool[l], pool_scale[l])
        mix2 = jnp.concatenate([a2.astype(z2.dtype), z2], axis=-1) @ w_out[l]
        hs_new = layer_norm(DN_ALPHA * hs + mix2, ln1_g[l], ln1_b[l])
        ks_l.append(k2)
        vs_l.append(v2)
        ps_l.append(u2_ext[:, -POOL_STATE:])
        fp = moe(hp_new, w_router[l], b_router[l], w_gate_up[l], b_gate_up[l], w_down[l], b_down[l])
        hp = layer_norm(DN_ALPHA * hp_new + fp, ln2_g[l], ln2_b[l])
        fs = moe(hs_new, w_router[l], b_router[l], w_gate_up[l], b_gate_up[l], w_down[l], b_down[l])
        hs = layer_norm(DN_ALPHA * hs_new + fs, ln2_g[l], ln2_b[l])
    k_prompt = jnp.stack(kp_l, axis=0)
    v_prompt = jnp.stack(vp_l, axis=0)
    pool_prompt = jnp.stack(pp_l, axis=0)
    k_sample = jnp.stack(ks_l, axis=0)
    v_sample = jnp.stack(vs_l, axis=0)
    pool_sample = jnp.stack(ps_l, axis=0)
    return (hp, hs, k_prompt, v_prompt, pool_prompt, k_sample, v_sample, pool_sample)
```

```python
import functools
import math

import jax
import jax.numpy as jnp
from jax import lax
from jax.experimental import pallas as pl
from jax.experimental.pallas import tpu as pltpu

F32 = jnp.float32
BF16 = jnp.bfloat16
I32 = jnp.int32

D_MODEL = 1024
PAGE_SIZE = 128
N_HEADS = 4
V_DIM = 128
QK_DIM = 64
ATTN_WIDTH = N_HEADS * V_DIM
POOL_WIDTH = 512
QK_SCALE = QK_DIM ** -0.5
IN_WIDTH = 4 * ATTN_WIDTH
POOL_WINDOWS = (2, 4, 8, 16)
POOL_GROUP = 128
POOL_STATE = 15
NUM_BUCKETS = 32
MAX_DISTANCE = 128
N_EXPERTS = 32
TOP_K = 4
D_FF = D_MODEL
SWIGLU_LIMIT = 7.0
SWIGLU_ALPHA = 1.702
DEPTH = 1
DN_ALPHA = (2.0 * DEPTH) ** 0.25
LN_EPS = 1e-5
LAMBDA_INIT = 0.8 - 0.6 * math.exp(-0.3 * 0)

LANES = 128
NEG_BIG = -1e30
VMEM_LIMIT = 56 * 1024 * 1024


def _ln(x, g, b):
    mu = jnp.mean(x, axis=-1, keepdims=True)
    xc = x - mu
    var = jnp.mean(xc * xc, axis=-1, keepdims=True)
    return xc * lax.rsqrt(var + LN_EPS) * g + b


def _lam(lp_ref):
    lp = lp_ref[...]
    s1 = jnp.sum(lp[0:1, :] * lp[1:2, :], axis=-1, keepdims=True)
    s2 = jnp.sum(lp[2:3, :] * lp[3:4, :], axis=-1, keepdims=True)
    return jnp.exp(s1) - jnp.exp(s2) + LAMBDA_INIT


def _inproj_prompt_kernel(x_ref, g_ref, b_ref, w_ref,
                          k_ref, v_ref, u_ref, qt_ref, kb_ref, vt_ref):
    h = _ln(x_ref[...], g_ref[...], b_ref[...]).astype(BF16)
    p = jnp.dot(h, w_ref[...], preferred_element_type=F32)
    q = p[:, 0:512] * QK_SCALE
    k = p[:, 512:1024]
    v = p[:, 1024:1536]
    k_ref[...] = k
    v_ref[...] = v
    u_ref[...] = p[:, 1536:2048]
    kb_ref[...] = k.astype(BF16)
    qt_ref[...] = q.T.astype(BF16)
    vt_ref[...] = v.T.astype(BF16)


def _inproj_sample_kernel(x_ref, g_ref, b_ref, w_ref, q_ref, k_ref, v_ref, u_ref):
    h = _ln(x_ref[...], g_ref[...], b_ref[...]).astype(BF16)
    p = jnp.dot(h, w_ref[...], preferred_element_type=F32)
    q_ref[...] = p[:, 0:512] * QK_SCALE
    k_ref[...] = p[:, 512:1024]
    v_ref[...] = p[:, 1024:1536]
    u_ref[...] = p[:, 1536:2048]


def _inproj(x, g, b, w16, *, tm, prompt):
    t = x.shape[0]
    row = lambda i: (i, 0)
    col = lambda i: (0, i)
    const = lambda i: (0, 0)
    in_specs = [pl.BlockSpec((tm, D_MODEL), row),
                pl.BlockSpec((1, D_MODEL), const),
                pl.BlockSpec((1, D_MODEL), const),
                pl.BlockSpec((D_MODEL, IN_WIDTH), const)]
    f32_out = jax.ShapeDtypeStruct((t, 512), F32)
    if prompt:
        out_shape = [f32_out, f32_out, f32_out,
                     jax.ShapeDtypeStruct((512, t), BF16),
                     jax.ShapeDtypeStruct((t, 512), BF16),
                     jax.ShapeDtypeStruct((512, t), BF16)]
        out_specs = [pl.BlockSpec((tm, 512), row)] * 3 + [
            pl.BlockSpec((512, tm), col), pl.BlockSpec((tm, 512), row),
            pl.BlockSpec((512, tm), col)]
        body = _inproj_prompt_kernel
    else:
        out_shape = [f32_out] * 4
        out_specs = [pl.BlockSpec((tm, 512), row)] * 4
        body = _inproj_sample_kernel
    return pl.pallas_call(
        body, grid=(t // tm,), in_specs=in_specs, out_specs=out_specs,
        out_shape=out_shape,
        compiler_params=pltpu.CompilerParams(
            dimension_semantics=("arbitrary",), vmem_limit_bytes=VMEM_LIMIT),
        name="inproj_prompt" if prompt else "inproj_sample",
    )(x, g, b, w16)


def _bucket(dist):
    max_exact = NUM_BUCKETS // 2
    df = jnp.maximum(dist, 1).astype(F32)
    large = max_exact + (jnp.log(df / max_exact) / math.log(MAX_DISTANCE / max_exact)
                         * (NUM_BUCKETS - max_exact)).astype(I32)
    large = jnp.minimum(large, NUM_BUCKETS - 1)
    return jnp.where(dist < max_exact, dist, large)


def _bias_rel(table, dist):
    far = table[NUM_BUCKETS - 1].astype(F32)
    b = jnp.take(table.astype(F32), _bucket(dist), axis=0) - far
    return jnp.moveaxis(b, -1, 0)


def _attn_prompt_kernel(qi_tab, ki_tab, qt_ref, k_ref, vt_ref, bias_ref, lp_ref, g_ref,
                        o_ref, qm_s, acc_s, m_s, l_s, *, tq):
    i = pl.program_id(0)
    qi = qi_tab[i]
    ki = ki_tab[i]

    @pl.when(ki == 0)
    def _init():
        rowid = lax.broadcasted_iota(I32, (V_DIM, tq), 0)
        for h in range(N_HEADS):
            qh = qt_ref[h * 128:(h + 1) * 128, :].astype(F32)
            qm_s[2 * h] = jnp.where(rowid < QK_DIM, qh, 0.0).astype(BF16)
            qm_s[2 * h + 1] = jnp.where(rowid >= QK_DIM, qh, 0.0).astype(BF16)
        acc_s[...] = jnp.zeros_like(acc_s)
        l_s[...] = jnp.zeros_like(l_s)
        m_s[...] = jnp.full_like(m_s, NEG_BIG)

    def step(with_bias):
        for h in range(N_HEADS):
            kh = k_ref[:, h * 128:(h + 1) * 128]
            vh = vt_ref[h * 128:(h + 1) * 128, :]
            for c in range(2):
                j = 2 * h + c
                s = jnp.dot(kh, qm_s[j], preferred_element_type=F32)
                if with_bias:
                    s = s + bias_ref[0, h]
                m_old = m_s[j:j + 1, :]
                m_new = jnp.maximum(m_old, jnp.max(s, axis=0, keepdims=True))
                alpha = jnp.exp(m_old - m_new)
                p = jnp.exp(s - m_new)
                l_s[j:j + 1, :] = alpha * l_s[j:j + 1, :] + jnp.sum(p, axis=0, keepdims=True)
                acc_s[j] = alpha * acc_s[j] + jnp.dot(vh, p.astype(BF16),
                                                      preferred_element_type=F32)
                m_s[j:j + 1, :] = m_new

    @pl.when(ki < qi - 1)
    def _far():
        step(False)

    @pl.when(ki >= qi - 1)
    def _near():
        step(True)

    @pl.when(ki == qi)
    def _fin():
        lam = _lam(lp_ref)
        for h in range(N_HEADS):
            o1 = acc_s[2 * h] * (1.0 / l_s[2 * h:2 * h + 1, :])
            o2 = acc_s[2 * h + 1] * (1.0 / l_s[2 * h + 1:2 * h + 2, :])
            a = o1 - lam * o2
            ms = jnp.mean(a * a, axis=0, keepdims=True)
            a = a * lax.rsqrt(ms + LN_EPS) * g_ref[...] * (1.0 - LAMBDA_INIT)
            o_ref[:, h * 128:(h + 1) * 128] = a.T.astype(BF16)


def _attn_prompt(qt, kb, vt, table, lam_params, subln_g, *, tq):
    t = kb.shape[0]
    nq = t // tq
    pairs = [(a, b) for a in range(nq) for b in range(a + 1)]
    qi_tab = jnp.asarray([p[0] for p in pairs], I32)
    ki_tab = jnp.asarray([p[1] for p in pairs], I32)
    kk = jnp.arange(tq)[:, None]
    qq = jnp.arange(tq)[None, :]
    sub = _bias_rel(table, tq + qq - kk)
    diag = jnp.where(qq >= kk, _bias_rel(table, jnp.maximum(qq - kk, 0)), NEG_BIG)
    bias = jnp.stack([sub, diag], axis=0)
    grid_spec = pltpu.PrefetchScalarGridSpec(
        num_scalar_prefetch=2, grid=(len(pairs),),
        in_specs=[
            pl.BlockSpec((512, tq), lambda i, qt_, kt_: (0, qt_[i])),
            pl.BlockSpec((tq, 512), lambda i, qt_, kt_: (kt_[i], 0)),
            pl.BlockSpec((512, tq), lambda i, qt_, kt_: (0, kt_[i])),
            pl.BlockSpec((1, N_HEADS, tq, tq),
                         lambda i, qt_, kt_: (jnp.where(kt_[i] == qt_[i], 1, 0), 0, 0, 0)),
            pl.BlockSpec((4, QK_DIM), lambda i, qt_, kt_: (0, 0)),
            pl.BlockSpec((V_DIM, 1), lambda i, qt_, kt_: (0, 0)),
        ],
        out_specs=pl.BlockSpec((tq, 512), lambda i, qt_, kt_: (qt_[i], 0)),
        scratch_shapes=[pltpu.VMEM((8, V_DIM, tq), BF16),
                        pltpu.VMEM((8, V_DIM, tq), F32),
                        pltpu.VMEM((8, tq), F32),
                        pltpu.VMEM((8, tq), F32)])
    return pl.pallas_call(
        functools.partial(_attn_prompt_kernel, tq=tq),
        grid_spec=grid_spec,
        out_shape=jax.ShapeDtypeStruct((t, 512), BF16),
        compiler_params=pltpu.CompilerParams(
            dimension_semantics=("arbitrary",), vmem_limit_bytes=VMEM_LIMIT),
        name="attn_prompt",
    )(qi_tab, ki_tab, qt, kb, vt, bias, lam_params, subln_g.reshape(V_DIM, 1))


def _attn_sample_kernel(pt_ref, q_ref, *refs, pg, n_chunks):
    k_refs = refs[:pg]
    v_refs = refs[pg:2 * pg]
    (kn_ref, vn_ref, bl_ref, bn_ref, lp_ref, g_ref,
     o_ref, acc_s, m_s, l_s) = refs[2 * pg:]
    c = pl.program_id(1)
    last = c == n_chunks - 1

    @pl.when(c == 0)
    def _init():
        acc_s[...] = jnp.zeros_like(acc_s)
        l_s[...] = jnp.zeros_like(l_s)
        m_s[...] = jnp.full_like(m_s, NEG_BIG)

    qr = q_ref[0]
    nt = (((1,), (1,)), ((), ()))
    s_list = []
    for j in range(pg):
        kj = k_refs[j][0].astype(BF16)
        s_list.append(lax.dot_general(qr, kj, nt, preferred_element_type=F32))
    s_list[-1] = s_list[-1] + jnp.where(last, bl_ref[...], 0.0)
    s = jnp.concatenate(s_list, axis=1)
    m_old = m_s[...]
    m_new = jnp.maximum(m_old, jnp.max(s, axis=1, keepdims=True))
    alpha = jnp.exp(m_old - m_new)
    p = jnp.exp(s - m_new)
    l_s[...] = alpha * l_s[...] + jnp.sum(p, axis=1, keepdims=True)
    pv = jnp.zeros(acc_s.shape, F32)
    for j in range(pg):
        pv = pv + jnp.dot(p[:, j * 128:(j + 1) * 128].astype(BF16),
                          v_refs[j][0].astype(BF16), preferred_element_type=F32)
    acc_s[...] = alpha * acc_s[...] + pv
    m_s[...] = m_new

    @pl.when(last)
    def _fin():
        kn = kn_ref[0].astype(BF16).astype(F32)
        vn = vn_ref[0].astype(BF16).astype(F32)
        s_new = lax.dot_general(qr.astype(F32), kn, nt,
                                preferred_element_type=F32) + bn_ref[...]
        m0 = m_s[...]
        m1 = jnp.maximum(m0, jnp.max(s_new, axis=1, keepdims=True))
        a1 = jnp.exp(m0 - m1)
        p_new = jnp.exp(s_new - m1)
        l1 = a1 * l_s[...] + jnp.sum(p_new, axis=1, keepdims=True)
        p_new = p_new.astype(BF16).astype(F32)
        acc = a1 * acc_s[...] + jnp.dot(p_new, vn, preferred_element_type=F32)
        o = acc * (1.0 / l1)
        lam = _lam(lp_ref)
        for h in range(N_HEADS):
            blk = o[h * 16:(h + 1) * 16, h * 128:(h + 1) * 128]
            a = blk[0:8, :] - lam * blk[8:16, :]
            ms = jnp.mean(a * a, axis=-1, keepdims=True)
            a = a * lax.rsqrt(ms + LN_EPS) * g_ref[...] * (1.0 - LAMBDA_INIT)
            o_ref[0, :, h * 128:(h + 1) * 128] = a


def _attn_sample(q, k_new, v_new, cache_k, cache_v, page_table, table, lam_params, subln_g,
                 *, pg):
    nb, n_pages = page_table.shape
    t = q.shape[0] // nb
    n_chunks = n_pages // pg
    past = n_pages * PAGE_SIZE
    n_pool = cache_k.shape[0]
    ck = cache_k.reshape(n_pool, PAGE_SIZE, 512)
    cv = cache_v.reshape(n_pool, PAGE_SIZE, 512)
    q5 = q.reshape(nb, t, 8, QK_DIM)
    eye = jnp.eye(8, dtype=F32)
    qrows = jnp.einsum('bted,ef->betfd', q5, eye).reshape(nb, 8 * t, 512).astype(BF16)
    tt = jnp.arange(t)
    jj = jnp.arange(PAGE_SIZE)
    d_last = (past + tt[:, None]) - (past - PAGE_SIZE + jj[None, :])
    b_last = _bias_rel(table, d_last)
    b_last = jnp.broadcast_to(b_last[:, None], (N_HEADS, 2, t, PAGE_SIZE)).reshape(8 * t, PAGE_SIZE)
    d_new = tt[:, None] - tt[None, :]
    b_new = jnp.where(d_new >= 0, _bias_rel(table, jnp.maximum(d_new, 0)), NEG_BIG)
    b_new = jnp.broadcast_to(b_new[:, None], (N_HEADS, 2, t, t)).reshape(8 * t, t)

    def page_map(j):
        return lambda b, c, pt: (pt[b, c * pg + j], 0, 0)

    seq3 = lambda b, c, pt: (b, 0, 0)
    const2 = lambda b, c, pt: (0, 0)
    in_specs = ([pl.BlockSpec((1, 8 * t, 512), seq3)]
                + [pl.BlockSpec((1, PAGE_SIZE, 512), page_map(j)) for j in range(pg)]
                + [pl.BlockSpec((1, PAGE_SIZE, 512), page_map(j)) for j in range(pg)]
                + [pl.BlockSpec((1, t, 512), seq3), pl.BlockSpec((1, t, 512), seq3),
                   pl.BlockSpec((8 * t, PAGE_SIZE), const2),
                   pl.BlockSpec((8 * t, t), const2),
                   pl.BlockSpec((4, QK_DIM), const2),
                   pl.BlockSpec((1, V_DIM), const2)])
    grid_spec = pltpu.PrefetchScalarGridSpec(
        num_scalar_prefetch=1, grid=(nb, n_chunks), in_specs=in_specs,
        out_specs=pl.BlockSpec((1, t, 512), seq3),
        scratch_shapes=[pltpu.VMEM((8 * t, 512), F32),
                        pltpu.VMEM((8 * t, 1), F32),
                        pltpu.VMEM((8 * t, 1), F32)])
    out = pl.pallas_call(
        functools.partial(_attn_sample_kernel, pg=pg, n_chunks=n_chunks),
        grid_spec=grid_spec,
        out_shape=jax.ShapeDtypeStruct((nb, t, 512), F32),
        compiler_params=pltpu.CompilerParams(
            dimension_semantics=("arbitrary", "arbitrary"), vmem_limit_bytes=VMEM_LIMIT),
        name="attn_sample",
    )(page_table, qrows, *([ck] * pg), *([cv] * pg),
      k_new.reshape(nb, t, 512), v_new.reshape(nb, t, 512),
      b_last, b_new, lam_params, subln_g.reshape(1, V_DIM))
    return out.reshape(nb * t, 512)


def _mix_tail(x, a16, pooled, u_now, wpool_ref, ps_ref, wout_ref, ging_ref, binb_ref,
              g1_ref, b1_ref, wr_ref, br_ref,
              h1_ref, e_ref, gt_ref, rk_ref, cnt_ref, carry_s):
    i = pl.program_id(0)
    tm = x.shape[0]
    hp = _ln(x, ging_ref[...], binb_ref[...])
    pooled = pooled - u_now
    zs = [jnp.dot(pooled[:, g * 128:(g + 1) * 128].astype(BF16), wpool_ref[g],
                  preferred_element_type=F32) for g in range(4)]
    z = jnp.concatenate(zs, axis=1) * ps_ref[...]
    mix = (jnp.dot(a16, wout_ref[0:512, :], preferred_element_type=F32)
           + jnp.dot(z.astype(BF16), wout_ref[512:1024, :], preferred_element_type=F32))
    h1 = _ln(DN_ALPHA * hp + mix, g1_ref[...], b1_ref[...])
    h1_ref[...] = h1

    x_hi = h1.astype(BF16)
    x_lo = (h1 - x_hi.astype(F32)).astype(BF16)
    w = wr_ref[...]
    w_hi = w.astype(BF16)
    w_lo = (w - w_hi.astype(F32)).astype(BF16)
    logits = (jnp.dot(x_hi, w_hi, preferred_element_type=F32)
              + jnp.dot(x_lo, w_hi, preferred_element_type=F32)
              + jnp.dot(x_hi, w_lo, preferred_element_type=F32)) + br_ref[...]
    lane = lax.broadcasted_iota(I32, (tm, LANES), 1).astype(F32)
    cur = jnp.where(lane < N_EXPERTS, logits, NEG_BIG)
    vals, idxs = [], []
    for _ in range(TOP_K):
        mx = jnp.max(cur, axis=1, keepdims=True)
        idx = jnp.min(jnp.where(cur == mx, lane, float(LANES)), axis=1, keepdims=True)
        vals.append(mx)
        idxs.append(idx)
        cur = jnp.where(lane == idx, NEG_BIG, cur)
    ex = [jnp.exp(v - vals[0]) for v in vals]
    den = ex[0] + ex[1] + ex[2] + ex[3]
    gates = [e / den for e in ex]

    @pl.when(i == 0)
    def _zero():
        carry_s[...] = jnp.zeros_like(carry_s)

    sel = [lane == idx for idx in idxs]
    member = jnp.zeros((tm, LANES), F32)
    for s_ in sel:
        member = jnp.where(s_, 1.0, member)
    r_id = lax.broadcasted_iota(I32, (tm, tm), 0)
    c_id = lax.broadcasted_iota(I32, (tm, tm), 1)
    lower = jnp.where(c_id < r_id, 1.0, 0.0).astype(BF16)
    pref = jnp.dot(lower, member.astype(BF16), preferred_element_type=F32) + carry_s[...]
    ranks = [jnp.sum(jnp.where(s_, pref, 0.0), axis=1, keepdims=True) for s_ in sel]
    carry_s[...] = carry_s[...] + jnp.sum(member, axis=0, keepdims=True)
    cnt_ref[...] = carry_s[...].astype(I32)

    e_out = jnp.zeros((tm, LANES), F32)
    g_out = jnp.zeros((tm, LANES), F32)
    r_out = jnp.zeros((tm, LANES), F32)
    for k in range(TOP_K):
        at_k = lane == float(k)
        e_out = jnp.where(at_k, idxs[k], e_out)
        g_out = jnp.where(at_k, gates[k], g_out)
        r_out = jnp.where(at_k, ranks[k], r_out)
    e_ref[...] = e_out.astype(I32)
    gt_ref[...] = g_out
    rk_ref[...] = r_out.astype(I32)


def _mix_prompt_kernel(x_ref, a_ref, u_ref, up_ref, *refs):
    ext_s = refs[-1]
    refs = refs[:-1]
    i = pl.program_id(0)
    tm = x_ref.shape[0]
    ext_s[0:16, :] = jnp.where(i == 0, 0.0, up_ref[...])
    u = u_ref[...]
    ext_s[16:16 + tm, :] = u
    pos = i * tm + lax.broadcasted_iota(I32, (tm, 1), 0)
    means = []
    for g, w in enumerate(POOL_WINDOWS):
        sl = slice(g * POOL_GROUP, (g + 1) * POOL_GROUP)
        s = u[:, sl]
        for j in range(1, w):
            s = s + ext_s[16 - j:16 - j + tm, sl]
        cnt = jnp.minimum(pos + 1, w).astype(F32)
        means.append(s / cnt)
    pooled = jnp.concatenate(means, axis=1)
    _mix_tail(x_ref[...], a_ref[...], pooled, u, *refs)


def _mix_sample_kernel(x_ref, a_ref, ue_ref, *refs):
    bs = ue_ref.shape[0]
    t = ue_ref.shape[1] - 16
    u = ue_ref[:, 16:16 + t, :]
    means = []
    for g, w in enumerate(POOL_WINDOWS):
        sl = slice(g * POOL_GROUP, (g + 1) * POOL_GROUP)
        s = u[:, :, sl]
        for j in range(1, w):
            s = s + ue_ref[:, 16 - j:16 - j + t, sl]
        means.append(s / float(w))
    pooled = jnp.concatenate(means, axis=2).reshape(bs * t, POOL_WIDTH)
    _mix_tail(x_ref[...], a_ref[...].astype(BF16), pooled, u.reshape(bs * t, POOL_WIDTH), *refs)


def _mix(x, a, u_or_ext, params, *, tm, prompt):
    t = x.shape[0]
    row = lambda i: (i, 0)
    const = lambda i: (0, 0)
    const3 = lambda i: (0, 0, 0)
    (wpool16, pscale, wout16, ln_in_g, ln_in_b, ln1_g, ln1_b, wr_pad, br_pad) = params
    param_specs = [pl.BlockSpec((4, POOL_GROUP, POOL_GROUP), const3),
                   pl.BlockSpec((1, POOL_WIDTH), const),
                   pl.BlockSpec((D_MODEL, D_MODEL), const),
                   pl.BlockSpec((1, D_MODEL), const), pl.BlockSpec((1, D_MODEL), const),
                   pl.BlockSpec((1, D_MODEL), const), pl.BlockSpec((1, D_MODEL), const),
                   pl.BlockSpec((D_MODEL, LANES), const), pl.BlockSpec((1, LANES), const)]
    if prompt:
        data = [x, a, u_or_ext, u_or_ext]
        data_specs = [pl.BlockSpec((tm, D_MODEL), row), pl.BlockSpec((tm, 512), row),
                      pl.BlockSpec((tm, 512), row),
                      pl.BlockSpec((16, 512), lambda i: (jnp.maximum(i * (tm // 16) - 1, 0), 0))]
        body = _mix_prompt_kernel
        scratch = [pltpu.VMEM((1, LANES), F32), pltpu.VMEM((16 + tm, 512), F32)]
    else:
        seq_t = u_or_ext.shape[1] - 16
        bs = tm // seq_t
        data = [x, a, u_or_ext]
        data_specs = [pl.BlockSpec((tm, D_MODEL), row), pl.BlockSpec((tm, 512), row),
                      pl.BlockSpec((bs, 16 + seq_t, 512), lambda i: (i, 0, 0))]
        body = _mix_sample_kernel
        scratch = [pltpu.VMEM((1, LANES), F32)]
    out_shape = [jax.ShapeDtypeStruct((t, D_MODEL), F32),
                 jax.ShapeDtypeStruct((t, LANES), I32),
                 jax.ShapeDtypeStruct((t, LANES), F32),
                 jax.ShapeDtypeStruct((t, LANES), I32),
                 jax.ShapeDtypeStruct((1, LANES), I32)]
    out_specs = [pl.BlockSpec((tm, D_MODEL), row), pl.BlockSpec((tm, LANES), row),
                 pl.BlockSpec((tm, LANES), row), pl.BlockSpec((tm, LANES), row),
                 pl.BlockSpec((1, LANES), const)]
    return pl.pallas_call(
        body, grid=(t // tm,), in_specs=data_specs + param_specs, out_specs=out_specs,
        out_shape=out_shape, scratch_shapes=scratch,
        compiler_params=pltpu.CompilerParams(
            dimension_semantics=("arbitrary",), vmem_limit_bytes=VMEM_LIMIT),
        name="mix_prompt" if prompt else "mix_sample",
    )(*data, wpool16, pscale, wout16, ln_in_g, ln_in_b, ln1_g, ln1_b, wr_pad, br_pad)


def _gather_kernel(nu_ref, tok_ref, x_hbm, o_ref, buf, sem, *, tg):
    i = pl.program_id(0)

    def row_copy(r):
        return pltpu.make_async_copy(x_hbm.at[pl.ds(tok_ref[0, 0, r], 1)],
                                     buf.at[pl.ds(r, 1)], sem)

    @pl.when(i < nu_ref[0])
    def _run():
        def issue(r, carry):
            row_copy(r).start()
            return carry

        def drain(r, carry):
            row_copy(r).wait()
            return carry

        lax.fori_loop(0, tg, issue, 0)
        lax.fori_loop(0, tg, drain, 0)
        o_ref[...] = buf[...].astype(BF16)

    @pl.when(i >= nu_ref[0])
    def _skip():
        o_ref[...] = jnp.zeros_like(o_ref)


def _moe_gather(h1, row_tok, n_used, *, tg):
    rows = row_tok.shape[0]
    nblk = rows // tg
    grid_spec = pltpu.PrefetchScalarGridSpec(
        num_scalar_prefetch=1, grid=(nblk,),
        in_specs=[pl.BlockSpec((1, 1, tg), lambda i, nu: (i, 0, 0), memory_space=pltpu.SMEM),
                  pl.BlockSpec(memory_space=pl.ANY)],
        out_specs=pl.BlockSpec((tg, D_MODEL), lambda i, nu: (i, 0)),
        scratch_shapes=[pltpu.VMEM((tg, D_MODEL), F32), pltpu.SemaphoreType.DMA])
    return pl.pallas_call(
        functools.partial(_gather_kernel, tg=tg), grid_spec=grid_spec,
        out_shape=jax.ShapeDtypeStruct((rows, D_MODEL), BF16),
        compiler_params=pltpu.CompilerParams(
            dimension_semantics=("arbitrary",), vmem_limit_bytes=VMEM_LIMIT),
        name="moe_gather",
    )(n_used, row_tok.reshape(nblk, 1, tg), h1)


def _experts_kernel(be_ref, nu_ref, x_ref, wgu_ref, bgu_ref, wd_ref, bd_ref, o_ref,
                    wgu16_s, wd16_s):
    i = pl.program_id(0)
    e = be_ref[i]
    prev = be_ref[jnp.maximum(i - 1, 0)]

    @pl.when((i == 0) | (e != prev))
    def _cast():
        wgu16_s[...] = wgu_ref[0].astype(BF16)
        wd16_s[...] = wd_ref[0].astype(BF16)

    @pl.when(i < nu_ref[0])
    def _run():
        gu = jnp.dot(x_ref[...], wgu16_s[...], preferred_element_type=F32) + bgu_ref[0]
        gate = jnp.minimum(gu[:, 0:D_FF], SWIGLU_LIMIT)
        up = jnp.clip(gu[:, D_FF:2 * D_FF], -SWIGLU_LIMIT, SWIGLU_LIMIT)
        hdn = (up + 1.0) * (gate * jax.nn.sigmoid(SWIGLU_ALPHA * gate))
        o_ref[...] = jnp.dot(hdn.astype(BF16), wd16_s[...],
                             preferred_element_type=F32) + bd_ref[0]

    @pl.when(i >= nu_ref[0])
    def _skip():
        o_ref[...] = jnp.zeros_like(o_ref)


def _moe_experts(xs, block_e, n_used, w_gu, b_gu, w_down, b_down, *, tm):
    rows = xs.shape[0]
    nblk = rows // tm
    grid_spec = pltpu.PrefetchScalarGridSpec(
        num_scalar_prefetch=2, grid=(nblk,),
        in_specs=[pl.BlockSpec((tm, D_MODEL), lambda i, be, nu: (i, 0)),
                  pl.BlockSpec((1, D_MODEL, 2 * D_FF), lambda i, be, nu: (be[i], 0, 0)),
                  pl.BlockSpec((1, 1, 2 * D_FF), lambda i, be, nu: (be[i], 0, 0)),
                  pl.BlockSpec((1, D_FF, D_MODEL), lambda i, be, nu: (be[i], 0, 0)),
                  pl.BlockSpec((1, 1, D_MODEL), lambda i, be, nu: (be[i], 0, 0))],
        out_specs=pl.BlockSpec((tm, D_MODEL), lambda i, be, nu: (i, 0)),
        scratch_shapes=[pltpu.VMEM((D_MODEL, 2 * D_FF), BF16),
                        pltpu.VMEM((D_FF, D_MODEL), BF16)])
    return pl.pallas_call(
        _experts_kernel, grid_spec=grid_spec,
        out_shape=jax.ShapeDtypeStruct((rows, D_MODEL), F32),
        compiler_params=pltpu.CompilerParams(
            dimension_semantics=("arbitrary",), vmem_limit_bytes=VMEM_LIMIT),
        name="moe_experts",
    )(block_e, n_used, xs, w_gu, b_gu.reshape(N_EXPERTS, 1, 2 * D_FF),
      w_down, b_down.reshape(N_EXPERTS, 1, D_MODEL))


def _combine_kernel(dest_ref, yb_hbm, h1_ref, gt_ref, g2_ref, b2_ref, o_ref, buf, sem, *, tm):
    def row_copy(k, r):
        return pltpu.make_async_copy(yb_hbm.at[pl.ds(dest_ref[0, 0, k * tm + r], 1)],
                                     buf.at[k, pl.ds(r, 1)], sem)

    for k in range(TOP_K):
        def issue(r, carry, k=k):
            row_copy(k, r).start()
            return carry
        lax.fori_loop(0, tm, issue, 0)
    for k in range(TOP_K):
        def drain(r, carry, k=k):
            row_copy(k, r).wait()
            return carry
        lax.fori_loop(0, tm, drain, 0)
    gt = gt_ref[...]
    f = buf[0] * gt[:, 0:1]
    for k in range(1, TOP_K):
        f = f + buf[k] * gt[:, k:k + 1]
    o_ref[...] = _ln(DN_ALPHA * h1_ref[...] + f, g2_ref[...], b2_ref[...])


def _moe_combine(yb, dest, h1, gates, ln2_g, ln2_b, *, tm):
    t = h1.shape[0]
    nblk = t // tm
    dest3 = dest.reshape(nblk, tm, TOP_K).transpose(0, 2, 1).reshape(nblk, 1, TOP_K * tm)
    row = lambda i: (i, 0)
    const = lambda i: (0, 0)
    return pl.pallas_call(
        functools.partial(_combine_kernel, tm=tm), grid=(nblk,),
        in_specs=[pl.BlockSpec((1, 1, TOP_K * tm), lambda i: (i, 0, 0), memory_space=pltpu.SMEM),
                  pl.BlockSpec(memory_space=pl.ANY),
                  pl.BlockSpec((tm, D_MODEL), row),
                  pl.BlockSpec((tm, LANES), row),
                  pl.BlockSpec((1, D_MODEL), const), pl.BlockSpec((1, D_MODEL), const)],
        out_specs=pl.BlockSpec((tm, D_MODEL), row),
        out_shape=jax.ShapeDtypeStruct((t, D_MODEL), F32),
        scratch_shapes=[pltpu.VMEM((TOP_K, tm, D_MODEL), F32), pltpu.SemaphoreType.DMA],
        compiler_params=pltpu.CompilerParams(
            dimension_semantics=("arbitrary",), vmem_limit_bytes=VMEM_LIMIT),
        name="moe_combine",
    )(dest3, yb, h1, gates, ln2_g, ln2_b)


def _moe(h1, top_e, gates, rank, counts, w_gu, b_gu, w_down, b_down, ln2_g, ln2_b,
         *, tm_e, tg, tm_c):
    t = h1.shape[0]
    counts = counts[0, :N_EXPERTS]
    padded = (counts + tm_e - 1) // tm_e * tm_e
    pad_end = jnp.cumsum(padded)
    pad_start = pad_end - padded
    e4 = top_e[:, :TOP_K]
    dest = pad_start[e4] + rank[:, :TOP_K]
    nblk = (t * TOP_K) // tm_e + N_EXPERTS
    rows = nblk * tm_e
    block_e = jnp.minimum(
        jnp.searchsorted(pad_end, jnp.arange(nblk, dtype=I32) * tm_e, side='right'),
        N_EXPERTS - 1).astype(I32)
    n_rows_used = pad_end[-1]
    tok = jnp.repeat(jnp.arange(t, dtype=I32), TOP_K)
    row_tok = jnp.zeros((rows,), I32).at[dest.reshape(-1)].set(tok)
    nu_g = ((n_rows_used + tg - 1) // tg).astype(I32).reshape(1)
    nu_e = (n_rows_used // tm_e).astype(I32).reshape(1)
    xs = _moe_gather(h1, row_tok, nu_g, tg=tg)
    yb = _moe_experts(xs, block_e, nu_e, w_gu, b_gu, w_down, b_down, tm=tm_e)
    return _moe_combine(yb, dest.astype(I32), h1, gates, ln2_g, ln2_b, tm=tm_c)


def kernel(x_prompt, x_sample, cache_k, cache_v, state_pool, page_table, ln_in_g, ln_in_b, w_in, lambda_q1, lambda_k1, lambda_q2, lambda_k2, subln_g, rel_bias_table, w_pool, pool_scale, w_out, ln1_g, ln1_b, w_router, b_router, w_gate_up, b_gate_up, w_down, b_down, ln2_g, ln2_b):
    l = 0
    bsz, seq, _ = x_prompt.shape
    nb, dec_t, _ = x_sample.shape
    assert bsz == 1
    r2 = lambda a: a.reshape(1, -1)
    w_in16 = w_in[l].astype(BF16)
    lam_params = jnp.stack([lambda_q1[l], lambda_k1[l], lambda_q2[l], lambda_k2[l]]).astype(F32)
    wr_pad = jnp.pad(w_router[l], ((0, 0), (0, LANES - N_EXPERTS)))
    br_pad = jnp.pad(b_router[l], (0, LANES - N_EXPERTS)).reshape(1, LANES)
    mix_params = (w_pool[l].astype(BF16), r2(pool_scale[l]), w_out[l].astype(BF16),
                  r2(ln_in_g), r2(ln_in_b), r2(ln1_g[l]), r2(ln1_b[l]), wr_pad, br_pad)

    xp = x_prompt.reshape(seq, D_MODEL)
    k_p, v_p, u_p, qt_p, kb_p, vt_p = _inproj(xp, r2(ln_in_g), r2(ln_in_b), w_in16,
                                              tm=512, prompt=True)
    a_p = _attn_prompt(qt_p, kb_p, vt_p, rel_bias_table, lam_params, subln_g[l], tq=512)
    h1_p, e_p, g_p, rk_p, cnt_p = _mix(xp, a_p, u_p, mix_params, tm=512, prompt=True)
    y_p = _moe(h1_p, e_p, g_p, rk_p, cnt_p, w_gate_up[l], b_gate_up[l], w_down[l], b_down[l],
               r2(ln2_g[l]), r2(ln2_b[l]), tm_e=256, tg=1024, tm_c=256)

    xs = x_sample.reshape(nb * dec_t, D_MODEL)
    q_s, k_s, v_s, u_s = _inproj(xs, r2(ln_in_g), r2(ln_in_b), w_in16, tm=512, prompt=False)
    a_s = _attn_sample(q_s, k_s, v_s, cache_k[l], cache_v[l], page_table, rel_bias_table,
                       lam_params, subln_g[l], pg=16)
    u_ext = jnp.concatenate([jnp.zeros((nb, 1, POOL_WIDTH), F32), state_pool[l],
                             u_s.reshape(nb, dec_t, POOL_WIDTH)], axis=1)
    h1_s, e_s, g_s, rk_s, cnt_s = _mix(xs, a_s, u_ext, mix_params, tm=256, prompt=False)
    y_s = _moe(h1_s, e_s, g_s, rk_s, cnt_s, w_gate_up[l], b_gate_up[l], w_down[l], b_down[l],
               r2(ln2_g[l]), r2(ln2_b[l]), tm_e=128, tg=1024, tm_c=256)

    k_prompt = k_p.reshape(1, bsz, seq, N_HEADS, 2, QK_DIM)
    v_prompt = v_p.reshape(1, bsz, seq, N_HEADS, V_DIM)
    pool_prompt = u_p[seq - POOL_STATE:].reshape(1, bsz, POOL_STATE, POOL_WIDTH)
    k_sample = k_s.reshape(1, nb, dec_t, N_HEADS, 2, QK_DIM)
    v_sample = v_s.reshape(1, nb, dec_t, N_HEADS, V_DIM)
    pool_sample = u_ext[:, 16 + dec_t - POOL_STATE:].reshape(1, nb, POOL_STATE, POOL_WIDTH)
    return (y_p.reshape(bsz, seq, D_MODEL), y_s.reshape(nb, dec_t, D_MODEL),
            k_prompt, v_prompt, pool_prompt, k_sample, v_sample, pool_sample)
```

```python
import functools
import math

import jax
import jax.numpy as jnp
from jax import lax
from jax.experimental import pallas as pl
from jax.experimental.pallas import tpu as pltpu

F32 = jnp.float32
BF16 = jnp.bfloat16
I32 = jnp.int32

D_MODEL = 1024
PAGE_SIZE = 128
N_HEADS = 4
V_DIM = 128
QK_DIM = 64
ATTN_WIDTH = N_HEADS * V_DIM
POOL_WIDTH = 512
QK_SCALE = QK_DIM ** -0.5
IN_WIDTH = 4 * ATTN_WIDTH
POOL_WINDOWS = (2, 4, 8, 16)
POOL_GROUP = 128
POOL_STATE = 15
NUM_BUCKETS = 32
MAX_DISTANCE = 128
N_EXPERTS = 32
TOP_K = 4
D_FF = D_MODEL
SWIGLU_LIMIT = 7.0
SWIGLU_ALPHA = 1.702
DEPTH = 1
DN_ALPHA = (2.0 * DEPTH) ** 0.25
LN_EPS = 1e-5
LAMBDA_INIT = 0.8 - 0.6 * math.exp(-0.3 * 0)

LANES = 128
SUBLANES = 8
ROW_TILES = D_MODEL // LANES
NEG_BIG = -1e30
LOG2E = math.log2(math.e)
VMEM_LIMIT = 56 * 1024 * 1024
BF16_SUBLANES = 16
VT_ROWS = V_DIM + BF16_SUBLANES


def _ln(x, g, b):
    mu = jnp.mean(x, axis=-1, keepdims=True)
    xc = x - mu
    var = jnp.mean(xc * xc, axis=-1, keepdims=True)
    return xc * lax.rsqrt(var + LN_EPS) * g + b


def _lam(lp_ref):
    lp = lp_ref[...]
    s1 = jnp.sum(lp[0:1, :] * lp[1:2, :], axis=-1, keepdims=True)
    s2 = jnp.sum(lp[2:3, :] * lp[3:4, :], axis=-1, keepdims=True)
    return jnp.exp(s1) - jnp.exp(s2) + LAMBDA_INIT


def _to_row_tiles(ref, x):
    n = x.shape[0]
    for c in range(ROW_TILES):
        ref[pl.ds(c, n, stride=ROW_TILES), :] = x[:, c * LANES:(c + 1) * LANES]


def _from_row_tiles(ref, n, lead=()):
    return jnp.concatenate(
        [ref[lead + (pl.ds(c, n, stride=ROW_TILES), slice(None))] for c in range(ROW_TILES)], axis=1)


def _inproj_prompt_kernel(x_ref, g_ref, b_ref, w_ref,
                          k_ref, v_ref, u_ref, qt_ref, kb_ref, vt_ref):
    h = _ln(x_ref[...], g_ref[...], b_ref[...]).astype(BF16)
    p = jnp.dot(h, w_ref[...], preferred_element_type=F32)
    q = p[:, 0:512] * (QK_SCALE * LOG2E)
    k = p[:, 512:1024]
    v = p[:, 1024:1536]
    k_ref[...] = k
    v_ref[...] = v
    u_ref[...] = p[:, 1536:2048]
    kb_ref[...] = k.astype(BF16)
    qt_ref[...] = q.T.astype(BF16)
    vt = v.T.astype(BF16)
    ones = jnp.ones((BF16_SUBLANES, vt.shape[1]), BF16)
    for hd in range(N_HEADS):
        vt_ref[hd * VT_ROWS:hd * VT_ROWS + V_DIM, :] = vt[hd * V_DIM:(hd + 1) * V_DIM, :]
        vt_ref[hd * VT_ROWS + V_DIM:(hd + 1) * VT_ROWS, :] = ones


def _inproj_sample_kernel(x_ref, g_ref, b_ref, w_ref, q_ref, k_ref, v_ref, u_ref):
    h = _ln(x_ref[...], g_ref[...], b_ref[...]).astype(BF16)
    p = jnp.dot(h, w_ref[...], preferred_element_type=F32)
    q_ref[...] = p[:, 0:512] * (QK_SCALE * LOG2E)
    k_ref[...] = p[:, 512:1024]
    v_ref[...] = p[:, 1024:1536]
    u_ref[...] = p[:, 1536:2048]


def _inproj(x, g, b, w16, *, tm, prompt):
    t = x.shape[0]
    row = lambda i: (i, 0)
    col = lambda i: (0, i)
    const = lambda i: (0, 0)
    in_specs = [pl.BlockSpec((tm, D_MODEL), row),
                pl.BlockSpec((1, D_MODEL), const),
                pl.BlockSpec((1, D_MODEL), const),
                pl.BlockSpec((D_MODEL, IN_WIDTH), const)]
    f32_out = jax.ShapeDtypeStruct((t, 512), F32)
    if prompt:
        out_shape = [f32_out, f32_out, f32_out,
                     jax.ShapeDtypeStruct((512, t), BF16),
                     jax.ShapeDtypeStruct((t, 512), BF16),
                     jax.ShapeDtypeStruct((N_HEADS * VT_ROWS, t), BF16)]
        out_specs = [pl.BlockSpec((tm, 512), row)] * 3 + [
            pl.BlockSpec((512, tm), col), pl.BlockSpec((tm, 512), row),
            pl.BlockSpec((N_HEADS * VT_ROWS, tm), col)]
        body = _inproj_prompt_kernel
    else:
        out_shape = [f32_out] * 4
        out_specs = [pl.BlockSpec((tm, 512), row)] * 4
        body = _inproj_sample_kernel
    return pl.pallas_call(
        body, grid=(t // tm,), in_specs=in_specs, out_specs=out_specs,
        out_shape=out_shape,
        compiler_params=pltpu.CompilerParams(
            dimension_semantics=("arbitrary",), vmem_limit_bytes=VMEM_LIMIT),
        name="inproj_prompt" if prompt else "inproj_sample",
    )(x, g, b, w16)


def _bucket(dist):
    max_exact = NUM_BUCKETS // 2
    df = jnp.maximum(dist, 1).astype(F32)
    large = max_exact + (jnp.log(df / max_exact) / math.log(MAX_DISTANCE / max_exact)
                         * (NUM_BUCKETS - max_exact)).astype(I32)
    large = jnp.minimum(large, NUM_BUCKETS - 1)
    return jnp.where(dist < max_exact, dist, large)


def _bias_rel(table, dist):
    far = table[NUM_BUCKETS - 1].astype(F32)
    b = (jnp.take(table.astype(F32), _bucket(dist), axis=0) - far) * LOG2E
    return jnp.moveaxis(b, -1, 0)


def _attn_prompt_kernel(qi_tab, ki_tab, qt_ref, k_ref, vt_ref, wv_ref, lp_ref, g_ref,
                        o_ref, qm_s, acc_s, m_s, l_s, bias_s, *, tq):
    i = pl.program_id(0)
    qi = qi_tab[i]
    ki = ki_tab[i]

    @pl.when(i == 0)
    def _bias():
        for t in range(2 * N_HEADS):
            x = jnp.broadcast_to(wv_ref[t:t + 1, :], (tq, 2 * tq))
            r = pltpu.roll(x, 0, 1, stride=1, stride_axis=0)
            bias_s[t] = r[:, tq:2 * tq]

    @pl.when(ki == 0)
    def _init():
        rowid = lax.broadcasted_iota(I32, (V_DIM, tq), 0)
        for h in range(N_HEADS):
            qh = qt_ref[h * 128:(h + 1) * 128, :].astype(F32)
            qm_s[2 * h] = jnp.where(rowid < QK_DIM, qh, 0.0).astype(BF16)
            qm_s[2 * h + 1] = jnp.where(rowid >= QK_DIM, qh, 0.0).astype(BF16)
        acc_s[...] = jnp.zeros_like(acc_s)
        l_s[...] = jnp.zeros_like(l_s)
        m_s[...] = jnp.full_like(m_s, NEG_BIG)

    def step(bias_base):
        def scores(j):
            h = j // 2
            s = jnp.dot(k_ref[:, h * 128:(h + 1) * 128], qm_s[j],
                        preferred_element_type=F32)
            if bias_base is not None:
                s = s + bias_s[bias_base + h]
            return s

        pending = [scores(0), scores(1)]
        for j in range(2 * N_HEADS):
            h = j // 2
            s = pending.pop(0)
            m_old = m_s[j:j + 1, :]
            m_new = jnp.maximum(m_old, jnp.max(s, axis=0, keepdims=True))
            alpha = jnp.exp2(m_old - m_new)
            p16 = jnp.exp2((s - m_new).astype(BF16))
            m_s[j:j + 1, :] = m_new
            if j + 2 < 2 * N_HEADS:
                pending.append(scores(j + 2))
            pv = jnp.dot(vt_ref[h * VT_ROWS:(h + 1) * VT_ROWS, :], p16,
                         preferred_element_type=F32)
            l_s[j:j + 1, :] = alpha * l_s[j:j + 1, :] + pv[V_DIM:V_DIM + 1, :]
            acc_s[j] = alpha * acc_s[j] + pv[0:V_DIM, :]

    @pl.when(ki < qi - 1)
    def _far():
        step(None)

    @pl.when(ki == qi - 1)
    def _sub():
        step(0)

    @pl.when(ki == qi)
    def _diag():
        step(N_HEADS)
        lam = _lam(lp_ref)
        for h in range(N_HEADS):
            o1 = acc_s[2 * h] * (1.0 / l_s[2 * h:2 * h + 1, :])
            o2 = acc_s[2 * h + 1] * (1.0 / l_s[2 * h + 1:2 * h + 2, :])
            a = o1 - lam * o2
            ms = jnp.mean(a * a, axis=0, keepdims=True)
            a = a * lax.rsqrt(ms + LN_EPS) * g_ref[...] * (1.0 - LAMBDA_INIT)
            o_ref[:, h * 128:(h + 1) * 128] = a.T.astype(BF16)


def _attn_prompt(qt, kb, vt, table, lam_params, subln_g, *, tq):
    t = kb.shape[0]
    nq = t // tq
    pairs = [(a, b) for a in range(nq) for b in range(a + 1)]
    qi_tab = jnp.asarray([p[0] for p in pairs], I32)
    ki_tab = jnp.asarray([p[1] for p in pairs], I32)
    c = jnp.arange(2 * tq)
    w_sub = _bias_rel(table, c)
    w_diag = jnp.where(c >= tq, _bias_rel(table, jnp.maximum(c - tq, 0)), NEG_BIG)
    wvec = jnp.concatenate([w_sub, w_diag], axis=0)
    const = lambda i, qt_, kt_: (0, 0)
    grid_spec = pltpu.PrefetchScalarGridSpec(
        num_scalar_prefetch=2, grid=(len(pairs),),
        in_specs=[
            pl.BlockSpec((512, tq), lambda i, qt_, kt_: (0, qt_[i])),
            pl.BlockSpec((tq, 512), lambda i, qt_, kt_: (kt_[i], 0)),
            pl.BlockSpec((N_HEADS * VT_ROWS, tq), lambda i, qt_, kt_: (0, kt_[i])),
            pl.BlockSpec((2 * N_HEADS, 2 * tq), const),
            pl.BlockSpec((4, QK_DIM), const),
            pl.BlockSpec((V_DIM, 1), const),
        ],
        out_specs=pl.BlockSpec((tq, 512), lambda i, qt_, kt_: (qt_[i], 0)),
        scratch_shapes=[pltpu.VMEM((8, V_DIM, tq), BF16),
                        pltpu.VMEM((8, V_DIM, tq), F32),
                        pltpu.VMEM((8, tq), F32),
                        pltpu.VMEM((8, tq), F32),
                        pltpu.VMEM((2 * N_HEADS, tq, tq), F32)])
    return pl.pallas_call(
        functools.partial(_attn_prompt_kernel, tq=tq),
        grid_spec=grid_spec,
        out_shape=jax.ShapeDtypeStruct((t, 512), BF16),
        compiler_params=pltpu.CompilerParams(
            dimension_semantics=("arbitrary",), vmem_limit_bytes=VMEM_LIMIT),
        name="attn_prompt",
    )(qi_tab, ki_tab, qt, kb, vt, wvec, lam_params, subln_g.reshape(V_DIM, 1))


def _attn_sample_kernel(pt_ref, q_ref, kn_ref, vn_ref, bl_ref, bn_ref, lp_ref, g_ref,
                        kt_hbm, v_hbm, o_ref, kbuf, vbuf, ksem, vsem, acc_s, m_s, l_s,
                        *, pg, n_chunks, n_steps):
    step = pl.program_id(0)
    c = step % n_chunks
    slot = step % 2
    last = c == n_chunks - 1

    def page_copies(st, sl):
        bb = st // n_chunks
        cc = st % n_chunks
        cps = []
        for j in range(pg):
            page = pt_ref[bb, cc * pg + j]
            cps.append(pltpu.make_async_copy(kt_hbm.at[page], kbuf.at[sl, j], ksem.at[sl]))
            cps.append(pltpu.make_async_copy(v_hbm.at[page], vbuf.at[sl, j], vsem.at[sl]))
        return cps

    @pl.when(step == 0)
    def _prime():
        for cp in page_copies(0, 0):
            cp.start()

    @pl.when(step + 1 < n_steps)
    def _prefetch():
        for cp in page_copies(step + 1, 1 - slot):
            cp.start()

    @pl.when(c == 0)
    def _init():
        acc_s[...] = jnp.zeros_like(acc_s)
        l_s[...] = jnp.zeros_like(l_s)
        m_s[...] = jnp.full_like(m_s, NEG_BIG)

    for cp in page_copies(step, slot):
        cp.wait()

    qr = q_ref[0]
    s_list = [jnp.dot(qr, kbuf[slot, j].astype(BF16), preferred_element_type=F32)
              for j in range(pg)]
    s_list[-1] = s_list[-1] + jnp.where(last, bl_ref[...], 0.0)
    s = jnp.concatenate(s_list, axis=1)
    m_old = m_s[...]
    m_new = jnp.maximum(m_old, jnp.max(s, axis=1, keepdims=True))
    alpha = jnp.exp2(m_old - m_new)
    p = jnp.exp2(s - m_new)
    l_s[...] = alpha * l_s[...] + jnp.sum(p, axis=1, keepdims=True)
    pv = jnp.zeros(acc_s.shape, F32)
    for j in range(pg):
        vj = jnp.concatenate(
            [vbuf[slot, j, pl.ds(h, PAGE_SIZE, stride=N_HEADS), :] for h in range(N_HEADS)],
            axis=1).astype(BF16)
        pv = pv + jnp.dot(p[:, j * 128:(j + 1) * 128].astype(BF16), vj,
                          preferred_element_type=F32)
    acc_s[...] = alpha * acc_s[...] + pv
    m_s[...] = m_new

    @pl.when(last)
    def _fin():
        nt = (((1,), (1,)), ((), ()))
        kn = kn_ref[0].astype(BF16).astype(F32)
        vn = vn_ref[0].astype(BF16).astype(F32)
        s_new = lax.dot_general(qr.astype(F32), kn, nt,
                                preferred_element_type=F32) + bn_ref[...]
        m0 = m_s[...]
        m1 = jnp.maximum(m0, jnp.max(s_new, axis=1, keepdims=True))
        a1 = jnp.exp2(m0 - m1)
        p_new = jnp.exp2(s_new - m1)
        l1 = a1 * l_s[...] + jnp.sum(p_new, axis=1, keepdims=True)
        p_new = p_new.astype(BF16).astype(F32)
        acc = a1 * acc_s[...] + jnp.dot(p_new, vn, preferred_element_type=F32)
        o = acc * (1.0 / l1)
        lam = _lam(lp_ref)
        for h in range(N_HEADS):
            blk = o[h * 16:(h + 1) * 16, h * 128:(h + 1) * 128]
            a = blk[0:8, :] - lam * blk[8:16, :]
            ms = jnp.mean(a * a, axis=-1, keepdims=True)
            a = a * lax.rsqrt(ms + LN_EPS) * g_ref[...] * (1.0 - LAMBDA_INIT)
            o_ref[0, :, h * 128:(h + 1) * 128] = a


def _attn_sample(q, k_new, v_new, cache_k, cache_v, page_table, table, lam_params, subln_g,
                 *, pg):
    nb, n_pages = page_table.shape
    t = q.shape[0] // nb
    n_chunks = n_pages // pg
    n_steps = nb * n_chunks
    past = n_pages * PAGE_SIZE
    n_pool = cache_k.shape[0]
    kt = jnp.transpose(cache_k, (0, 2, 3, 4, 1)).reshape(n_pool, 512, PAGE_SIZE)
    v2 = cache_v.reshape(n_pool, PAGE_SIZE * N_HEADS, V_DIM)
    q5 = q.reshape(nb, t, 8, QK_DIM)
    eye = jnp.eye(8, dtype=F32)
    qrows = (jnp.transpose(q5, (0, 2, 1, 3))[:, :, :, None, :] * eye[None, :, None, :, None])
    qrows = qrows.reshape(nb, 8 * t, 512).astype(BF16)
    tt = jnp.arange(t)
    jj = jnp.arange(PAGE_SIZE)
    d_last = (past + tt[:, None]) - (past - PAGE_SIZE + jj[None, :])
    b_last = _bias_rel(table, d_last)
    b_last = jnp.broadcast_to(b_last[:, None], (N_HEADS, 2, t, PAGE_SIZE)).reshape(8 * t, PAGE_SIZE)
    d_new = tt[:, None] - tt[None, :]
    b_new = jnp.where(d_new >= 0, _bias_rel(table, jnp.maximum(d_new, 0)), NEG_BIG)
    b_new = jnp.broadcast_to(b_new[:, None], (N_HEADS, 2, t, t)).reshape(8 * t, t)

    seq3 = lambda s, pt: (s // n_chunks, 0, 0)
    const2 = lambda s, pt: (0, 0)
    in_specs = [pl.BlockSpec((1, 8 * t, 512), seq3),
                pl.BlockSpec((1, t, 512), seq3), pl.BlockSpec((1, t, 512), seq3),
                pl.BlockSpec((8 * t, PAGE_SIZE), const2),
                pl.BlockSpec((8 * t, t), const2),
                pl.BlockSpec((4, QK_DIM), const2),
                pl.BlockSpec((1, V_DIM), const2),
                pl.BlockSpec(memory_space=pl.ANY),
                pl.BlockSpec(memory_space=pl.ANY)]
    grid_spec = pltpu.PrefetchScalarGridSpec(
        num_scalar_prefetch=1, grid=(n_steps,), in_specs=in_specs,
        out_specs=pl.BlockSpec((1, t, 512), seq3),
        scratch_shapes=[pltpu.VMEM((2, pg, 512, PAGE_SIZE), F32),
                        pltpu.VMEM((2, pg, PAGE_SIZE * N_HEADS, V_DIM), F32),
                        pltpu.SemaphoreType.DMA((2,)),
                        pltpu.SemaphoreType.DMA((2,)),
                        pltpu.VMEM((8 * t, 512), F32),
                        pltpu.VMEM((8 * t, 1), F32),
                        pltpu.VMEM((8 * t, 1), F32)])
    out = pl.pallas_call(
        functools.partial(_attn_sample_kernel, pg=pg, n_chunks=n_chunks, n_steps=n_steps),
        grid_spec=grid_spec,
        out_shape=jax.ShapeDtypeStruct((nb, t, 512), F32),
        compiler_params=pltpu.CompilerParams(
            dimension_semantics=("arbitrary",), vmem_limit_bytes=VMEM_LIMIT),
        name="attn_sample",
    )(page_table, qrows, k_new.reshape(nb, t, 512), v_new.reshape(nb, t, 512),
      b_last, b_new, lam_params, subln_g.reshape(1, V_DIM), kt, v2)
    return out.reshape(nb * t, 512)


def _mix_tail(x, a16, pooled, u_now, wpool_ref, ps_ref, wout_ref, ging_ref, binb_ref,
              g1_ref, b1_ref, wr_ref, br_ref,
              h1_ref, h1t_ref, e_ref, gt_ref, rk_ref, cnt_ref, carry_s):
    i = pl.program_id(0)
    tm = x.shape[0]
    hp = _ln(x, ging_ref[...], binb_ref[...])
    pooled = pooled - u_now
    zs = [jnp.dot(pooled[:, g * 128:(g + 1) * 128].astype(BF16), wpool_ref[g],
                  preferred_element_type=F32) for g in range(4)]
    z = jnp.concatenate(zs, axis=1) * ps_ref[...]
    mix = (jnp.dot(a16, wout_ref[0:512, :], preferred_element_type=F32)
           + jnp.dot(z.astype(BF16), wout_ref[512:1024, :], preferred_element_type=F32))
    h1 = _ln(DN_ALPHA * hp + mix, g1_ref[...], b1_ref[...])
    h1_ref[...] = h1
    _to_row_tiles(h1t_ref, h1)

    x_hi = h1.astype(BF16)
    x_lo = (h1 - x_hi.astype(F32)).astype(BF16)
    w = wr_ref[...]
    w_hi = w.astype(BF16)
    w_lo = (w - w_hi.astype(F32)).astype(BF16)
    logits = (jnp.dot(x_hi, w_hi, preferred_element_type=F32)
              + jnp.dot(x_lo, w_hi, preferred_element_type=F32)
              + jnp.dot(x_hi, w_lo, preferred_element_type=F32)) + br_ref[...]
    lane = lax.broadcasted_iota(I32, (tm, LANES), 1).astype(F32)
    cur = jnp.where(lane < N_EXPERTS, logits, NEG_BIG)
    vals, idxs = [], []
    for _ in range(TOP_K):
        mx = jnp.max(cur, axis=1, keepdims=True)
        idx = jnp.min(jnp.where(cur == mx, lane, float(LANES)), axis=1, keepdims=True)
        vals.append(mx)
        idxs.append(idx)
        cur = jnp.where(lane == idx, NEG_BIG, cur)
    ex = [jnp.exp(v - vals[0]) for v in vals]
    den = ex[0] + ex[1] + ex[2] + ex[3]
    gates = [e / den for e in ex]

    @pl.when(i == 0)
    def _zero():
        carry_s[...] = jnp.zeros_like(carry_s)

    sel = [lane == idx for idx in idxs]
    member = jnp.zeros((tm, LANES), F32)
    for s_ in sel:
        member = jnp.where(s_, 1.0, member)
    r_id = lax.broadcasted_iota(I32, (tm, tm), 0)
    c_id = lax.broadcasted_iota(I32, (tm, tm), 1)
    lower = jnp.where(c_id < r_id, 1.0, 0.0).astype(BF16)
    pref = jnp.dot(lower, member.astype(BF16), preferred_element_type=F32) + carry_s[...]
    ranks = [jnp.sum(jnp.where(s_, pref, 0.0), axis=1, keepdims=True) for s_ in sel]
    carry_s[...] = carry_s[...] + jnp.sum(member, axis=0, keepdims=True)
    cnt_ref[...] = carry_s[...].astype(I32)

    e_out = jnp.zeros((tm, LANES), F32)
    g_out = jnp.zeros((tm, LANES), F32)
    r_out = jnp.zeros((tm, LANES), F32)
    for k in range(TOP_K):
        at_k = lane == float(k)
        e_out = jnp.where(at_k, idxs[k], e_out)
        g_out = jnp.where(at_k, gates[k], g_out)
        r_out = jnp.where(at_k, ranks[k], r_out)
    e_ref[...] = e_out.astype(I32)
    gt_ref[...] = g_out
    rk_ref[...] = r_out.astype(I32)


def _mix_prompt_kernel(x_ref, a_ref, u_ref, up_ref, *refs):
    ext_s = refs[-1]
    refs = refs[:-1]
    i = pl.program_id(0)
    tm = x_ref.shape[0]
    ext_s[0:16, :] = jnp.where(i == 0, 0.0, up_ref[...])
    u = u_ref[...]
    ext_s[16:16 + tm, :] = u
    pos = i * tm + lax.broadcasted_iota(I32, (tm, 1), 0)
    means = []
    for g, w in enumerate(POOL_WINDOWS):
        sl = slice(g * POOL_GROUP, (g + 1) * POOL_GROUP)
        s = u[:, sl]
        for j in range(1, w):
            s = s + ext_s[16 - j:16 - j + tm, sl]
        cnt = jnp.minimum(pos + 1, w).astype(F32)
        means.append(s / cnt)
    pooled = jnp.concatenate(means, axis=1)
    _mix_tail(x_ref[...], a_ref[...], pooled, u, *refs)


def _mix_sample_kernel(x_ref, a_ref, ue_ref, *refs):
    bs = ue_ref.shape[0]
    t = ue_ref.shape[1] - 16
    u = ue_ref[:, 16:16 + t, :]
    means = []
    for g, w in enumerate(POOL_WINDOWS):
        sl = slice(g * POOL_GROUP, (g + 1) * POOL_GROUP)
        s = u[:, :, sl]
        for j in range(1, w):
            s = s + ue_ref[:, 16 - j:16 - j + t, sl]
        means.append(s / float(w))
    pooled = jnp.concatenate(means, axis=2).reshape(bs * t, POOL_WIDTH)
    _mix_tail(x_ref[...], a_ref[...].astype(BF16), pooled, u.reshape(bs * t, POOL_WIDTH), *refs)


def _mix(x, a, u_or_ext, params, *, tm, prompt):
    t = x.shape[0]
    row = lambda i: (i, 0)
    const = lambda i: (0, 0)
    const3 = lambda i: (0, 0, 0)
    (wpool16, pscale, wout16, ln_in_g, ln_in_b, ln1_g, ln1_b, wr_pad, br_pad) = params
    param_specs = [pl.BlockSpec((4, POOL_GROUP, POOL_GROUP), const3),
                   pl.BlockSpec((1, POOL_WIDTH), const),
                   pl.BlockSpec((D_MODEL, D_MODEL), const),
                   pl.BlockSpec((1, D_MODEL), const), pl.BlockSpec((1, D_MODEL), const),
                   pl.BlockSpec((1, D_MODEL), const), pl.BlockSpec((1, D_MODEL), const),
                   pl.BlockSpec((D_MODEL, LANES), const), pl.BlockSpec((1, LANES), const)]
    if prompt:
        data = [x, a, u_or_ext, u_or_ext]
        data_specs = [pl.BlockSpec((tm, D_MODEL), row), pl.BlockSpec((tm, 512), row),
                      pl.BlockSpec((tm, 512), row),
                      pl.BlockSpec((16, 512), lambda i: (jnp.maximum(i * (tm // 16) - 1, 0), 0))]
        body = _mix_prompt_kernel
        scratch = [pltpu.VMEM((1, LANES), F32), pltpu.VMEM((16 + tm, 512), F32)]
    else:
        seq_t = u_or_ext.shape[1] - 16
        bs = tm // seq_t
        data = [x, a, u_or_ext]
        data_specs = [pl.BlockSpec((tm, D_MODEL), row), pl.BlockSpec((tm, 512), row),
                      pl.BlockSpec((bs, 16 + seq_t, 512), lambda i: (i, 0, 0))]
        body = _mix_sample_kernel
        scratch = [pltpu.VMEM((1, LANES), F32)]
    out_shape = [jax.ShapeDtypeStruct((t, D_MODEL), F32),
                 jax.ShapeDtypeStruct((t * ROW_TILES, LANES), F32),
                 jax.ShapeDtypeStruct((t, LANES), I32),
                 jax.ShapeDtypeStruct((t, LANES), F32),
                 jax.ShapeDtypeStruct((t, LANES), I32),
                 jax.ShapeDtypeStruct((1, LANES), I32)]
    out_specs = [pl.BlockSpec((tm, D_MODEL), row), pl.BlockSpec((tm * ROW_TILES, LANES), row),
                 pl.BlockSpec((tm, LANES), row),
                 pl.BlockSpec((tm, LANES), row), pl.BlockSpec((tm, LANES), row),
                 pl.BlockSpec((1, LANES), const)]
    return pl.pallas_call(
        body, grid=(t // tm,), in_specs=data_specs + param_specs, out_specs=out_specs,
        out_shape=out_shape, scratch_shapes=scratch,
        compiler_params=pltpu.CompilerParams(
            dimension_semantics=("arbitrary",), vmem_limit_bytes=VMEM_LIMIT),
        name="mix_prompt" if prompt else "mix_sample",
    )(*data, wpool16, pscale, wout16, ln_in_g, ln_in_b, ln1_g, ln1_b, wr_pad, br_pad)


def _tile_copy(src_hbm, src_row, dst, dst_row, sem):
    return pltpu.make_async_copy(
        src_hbm.at[pl.ds(pl.multiple_of(src_row * ROW_TILES, ROW_TILES), ROW_TILES)],
        dst.at[pl.ds(pl.multiple_of(dst_row * ROW_TILES, ROW_TILES), ROW_TILES)], sem)


def _gather_kernel(nu_ref, tok_ref, x_hbm, o_ref, buf, sem, *, tg):
    i = pl.program_id(0)

    def row_copy(r):
        return _tile_copy(x_hbm, tok_ref[0, 0, r], buf, r, sem)

    @pl.when(i < nu_ref[0])
    def _run():
        def issue(r, carry):
            row_copy(r).start()
            return carry

        def drain(r, carry):
            row_copy(r).wait()
            return carry

        lax.fori_loop(0, tg, issue, 0, unroll=8)
        lax.fori_loop(0, tg, drain, 0, unroll=8)
        o_ref[...] = _from_row_tiles(buf, tg).astype(BF16)

    @pl.when(i >= nu_ref[0])
    def _skip():
        o_ref[...] = jnp.zeros_like(o_ref)


def _moe_gather(h1t, row_tok, n_used, *, tg):
    rows = row_tok.shape[0]
    nblk = rows // tg
    grid_spec = pltpu.PrefetchScalarGridSpec(
        num_scalar_prefetch=1, grid=(nblk,),
        in_specs=[pl.BlockSpec((1, 1, tg), lambda i, nu: (i, 0, 0), memory_space=pltpu.SMEM),
                  pl.BlockSpec(memory_space=pl.ANY)],
        out_specs=pl.BlockSpec((tg, D_MODEL), lambda i, nu: (i, 0)),
        scratch_shapes=[pltpu.VMEM((tg * ROW_TILES, LANES), F32), pltpu.SemaphoreType.DMA])
    return pl.pallas_call(
        functools.partial(_gather_kernel, tg=tg), grid_spec=grid_spec,
        out_shape=jax.ShapeDtypeStruct((rows, D_MODEL), BF16),
        compiler_params=pltpu.CompilerParams(
            dimension_semantics=("arbitrary",), vmem_limit_bytes=VMEM_LIMIT),
        name="moe_gather",
    )(n_used, row_tok.reshape(nblk, 1, tg), h1t)


def _experts_kernel(be_ref, nu_ref, x_ref, wgu_ref, bgu_ref, wd_ref, bd_ref, o_ref,
                    wgu16_s, wd16_s):
    i = pl.program_id(0)
    e = be_ref[i]
    prev = be_ref[jnp.maximum(i - 1, 0)]

    @pl.when((i == 0) | (e != prev))
    def _cast():
        wgu16_s[...] = wgu_ref[0].astype(BF16)
        wd16_s[...] = wd_ref[0].astype(BF16)

    @pl.when(i < nu_ref[0])
    def _run():
        gu = jnp.dot(x_ref[...], wgu16_s[...], preferred_element_type=F32) + bgu_ref[0]
        gate = jnp.minimum(gu[:, 0:D_FF], SWIGLU_LIMIT)
        up = jnp.clip(gu[:, D_FF:2 * D_FF], -SWIGLU_LIMIT, SWIGLU_LIMIT)
        hdn = (up + 1.0) * (gate * jax.nn.sigmoid(SWIGLU_ALPHA * gate))
        y = jnp.dot(hdn.astype(BF16), wd16_s[...], preferred_element_type=F32) + bd_ref[0]
        _to_row_tiles(o_ref, y)

    @pl.when(i >= nu_ref[0])
    def _skip():
        o_ref[...] = jnp.zeros_like(o_ref)


def _moe_experts(xs, block_e, n_used, w_gu, b_gu, w_down, b_down, *, tm):
    rows = xs.shape[0]
    nblk = rows // tm
    grid_spec = pltpu.PrefetchScalarGridSpec(
        num_scalar_prefetch=2, grid=(nblk,),
        in_specs=[pl.BlockSpec((tm, D_MODEL), lambda i, be, nu: (i, 0)),
                  pl.BlockSpec((1, D_MODEL, 2 * D_FF), lambda i, be, nu: (be[i], 0, 0)),
                  pl.BlockSpec((1, 1, 2 * D_FF), lambda i, be, nu: (be[i], 0, 0)),
                  pl.BlockSpec((1, D_FF, D_MODEL), lambda i, be, nu: (be[i], 0, 0)),
                  pl.BlockSpec((1, 1, D_MODEL), lambda i, be, nu: (be[i], 0, 0))],
        out_specs=pl.BlockSpec((tm * ROW_TILES, LANES), lambda i, be, nu: (i, 0)),
        scratch_shapes=[pltpu.VMEM((D_MODEL, 2 * D_FF), BF16),
                        pltpu.VMEM((D_FF, D_MODEL), BF16)])
    return pl.pallas_call(
        _experts_kernel, grid_spec=grid_spec,
        out_shape=jax.ShapeDtypeStruct((rows * ROW_TILES, LANES), F32),
        compiler_params=pltpu.CompilerParams(
            dimension_semantics=("arbitrary",), vmem_limit_bytes=VMEM_LIMIT),
        name="moe_experts",
    )(block_e, n_used, xs, w_gu, b_gu.reshape(N_EXPERTS, 1, 2 * D_FF),
      w_down, b_down.reshape(N_EXPERTS, 1, D_MODEL))


def _combine_kernel(dest_ref, yb_hbm, h1_ref, gt_ref, g2_ref, b2_ref, o_ref, buf, sem, *, tm):
    def row_copy(k, r):
        return _tile_copy(yb_hbm, dest_ref[0, 0, k * tm + r], buf.at[k], r, sem)

    for k in range(TOP_K):
        def issue(r, carry, k=k):
            row_copy(k, r).start()
            return carry
        lax.fori_loop(0, tm, issue, 0, unroll=8)
    for k in range(TOP_K):
        def drain(r, carry, k=k):
            row_copy(k, r).wait()
            return carry
        lax.fori_loop(0, tm, drain, 0, unroll=8)
    gt = gt_ref[...]
    f = _from_row_tiles(buf, tm, (0,)) * gt[:, 0:1]
    for k in range(1, TOP_K):
        f = f + _from_row_tiles(buf, tm, (k,)) * gt[:, k:k + 1]
    o_ref[...] = _ln(DN_ALPHA * h1_ref[...] + f, g2_ref[...], b2_ref[...])


def _moe_combine(ybt, dest, h1, gates, ln2_g, ln2_b, *, tm):
    t = h1.shape[0]
    nblk = t // tm
    dest3 = dest.reshape(nblk, tm, TOP_K).transpose(0, 2, 1).reshape(nblk, 1, TOP_K * tm)
    row = lambda i: (i, 0)
    const = lambda i: (0, 0)
    return pl.pallas_call(
        functools.partial(_combine_kernel, tm=tm), grid=(nblk,),
        in_specs=[pl.BlockSpec((1, 1, TOP_K * tm), lambda i: (i, 0, 0), memory_space=pltpu.SMEM),
                  pl.BlockSpec(memory_space=pl.ANY),
                  pl.BlockSpec((tm, D_MODEL), row),
                  pl.BlockSpec((tm, LANES), row),
                  pl.BlockSpec((1, D_MODEL), const), pl.BlockSpec((1, D_MODEL), const)],
        out_specs=pl.BlockSpec((tm, D_MODEL), row),
        out_shape=jax.ShapeDtypeStruct((t, D_MODEL), F32),
        scratch_shapes=[pltpu.VMEM((TOP_K, tm * ROW_TILES, LANES), F32),
                        pltpu.SemaphoreType.DMA],
        compiler_params=pltpu.CompilerParams(
            dimension_semantics=("arbitrary",), vmem_limit_bytes=VMEM_LIMIT),
        name="moe_combine",
    )(dest3, ybt, h1, gates, ln2_g, ln2_b)


def _moe(h1, h1t, top_e, gates, rank, counts, w_gu, b_gu, w_down, b_down, ln2_g, ln2_b,
         *, tm_e, tg, tm_c):
    t = h1.shape[0]
    counts = counts[0, :N_EXPERTS]
    padded = (counts + tm_e - 1) // tm_e * tm_e
    pad_end = jnp.cumsum(padded)
    pad_start = pad_end - padded
    e4 = top_e[:, :TOP_K]
    onehot = e4[:, :, None] == jnp.arange(N_EXPERTS, dtype=I32)[None, None, :]
    dest = rank[:, :TOP_K] + jnp.sum(jnp.where(onehot, pad_start[None, None, :], 0), axis=-1)
    nblk = (t * TOP_K) // tm_e + N_EXPERTS
    rows = nblk * tm_e
    starts = jnp.arange(nblk, dtype=I32) * tm_e
    block_e = jnp.minimum(jnp.sum((pad_end[None, :] <= starts[:, None]).astype(I32), axis=1),
                          N_EXPERTS - 1).astype(I32)
    n_rows_used = pad_end[-1]
    tok = jnp.repeat(jnp.arange(t, dtype=I32), TOP_K)
    row_tok = jnp.zeros((rows,), I32).at[dest.reshape(-1)].set(tok, unique_indices=True)
    nu_g = ((n_rows_used + tg - 1) // tg).astype(I32).reshape(1)
    nu_e = (n_rows_used // tm_e).astype(I32).reshape(1)
    xs = _moe_gather(h1t, row_tok, nu_g, tg=tg)
    ybt = _moe_experts(xs, block_e, nu_e, w_gu, b_gu, w_down, b_down, tm=tm_e)
    return _moe_combine(ybt, dest.astype(I32), h1, gates, ln2_g, ln2_b, tm=tm_c)


def kernel(x_prompt, x_sample, cache_k, cache_v, state_pool, page_table, ln_in_g, ln_in_b, w_in, lambda_q1, lambda_k1, lambda_q2, lambda_k2, subln_g, rel_bias_table, w_pool, pool_scale, w_out, ln1_g, ln1_b, w_router, b_router, w_gate_up, b_gate_up, w_down, b_down, ln2_g, ln2_b):
    l = 0
    bsz, seq, _ = x_prompt.shape
    nb, dec_t, _ = x_sample.shape
    assert bsz == 1
    r2 = lambda a: a.reshape(1, -1)
    w_in16 = w_in[l].astype(BF16)
    lam_params = jnp.stack([lambda_q1[l], lambda_k1[l], lambda_q2[l], lambda_k2[l]]).astype(F32)
    wr_pad = jnp.pad(w_router[l], ((0, 0), (0, LANES - N_EXPERTS)))
    br_pad = jnp.pad(b_router[l], (0, LANES - N_EXPERTS)).reshape(1, LANES)
    mix_params = (w_pool[l].astype(BF16), r2(pool_scale[l]), w_out[l].astype(BF16),
                  r2(ln_in_g), r2(ln_in_b), r2(ln1_g[l]), r2(ln1_b[l]), wr_pad, br_pad)

    xp = x_prompt.reshape(seq, D_MODEL)
    k_p, v_p, u_p, qt_p, kb_p, vt_p = _inproj(xp, r2(ln_in_g), r2(ln_in_b), w_in16,
                                              tm=512, prompt=True)
    a_p = _attn_prompt(qt_p, kb_p, vt_p, rel_bias_table, lam_params, subln_g[l], tq=512)
    h1_p, h1t_p, e_p, g_p, rk_p, cnt_p = _mix(xp, a_p, u_p, mix_params, tm=512, prompt=True)
    y_p = _moe(h1_p, h1t_p, e_p, g_p, rk_p, cnt_p, w_gate_up[l], b_gate_up[l], w_down[l],
               b_down[l], r2(ln2_g[l]), r2(ln2_b[l]), tm_e=256, tg=1024, tm_c=256)

    xs = x_sample.reshape(nb * dec_t, D_MODEL)
    q_s, k_s, v_s, u_s = _inproj(xs, r2(ln_in_g), r2(ln_in_b), w_in16, tm=512, prompt=False)
    a_s = _attn_sample(q_s, k_s, v_s, cache_k[l], cache_v[l], page_table, rel_bias_table,
                       lam_params, subln_g[l], pg=32)
    u_ext = jnp.concatenate([jnp.zeros((nb, 1, POOL_WIDTH), F32), state_pool[l],
                             u_s.reshape(nb, dec_t, POOL_WIDTH)], axis=1)
    h1_s, h1t_s, e_s, g_s, rk_s, cnt_s = _mix(xs, a_s, u_ext, mix_params, tm=256, prompt=False)
    y_s = _moe(h1_s, h1t_s, e_s, g_s, rk_s, cnt_s, w_gate_up[l], b_gate_up[l], w_down[l],
               b_down[l], r2(ln2_g[l]), r2(ln2_b[l]), tm_e=128, tg=1024, tm_c=256)

    k_prompt = k_p.reshape(1, bsz, seq, N_HEADS, 2, QK_DIM)
    v_prompt = v_p.reshape(1, bsz, seq, N_HEADS, V_DIM)
    pool_prompt = u_p[seq - POOL_STATE:].reshape(1, bsz, POOL_STATE, POOL_WIDTH)
    k_sample = k_s.reshape(1, nb, dec_t, N_HEADS, 2, QK_DIM)
    v_sample = v_s.reshape(1, nb, dec_t, N_HEADS, V_DIM)
    pool_sample = u_ext[:, 16 + dec_t - POOL_STATE:].reshape(1, nb, POOL_STATE, POOL_WIDTH)
    return (y_p.reshape(bsz, seq, D_MODEL), y_s.reshape(nb, dec_t, D_MODEL),
            k_prompt, v_prompt, pool_prompt, k_sample, v_sample, pool_sample)
```

```python
import functools
import math

import jax
import jax.numpy as jnp
from jax import lax
from jax.experimental import pallas as pl
from jax.experimental.pallas import tpu as pltpu

F32 = jnp.float32
BF16 = jnp.bfloat16
I32 = jnp.int32

D_MODEL = 1024
PAGE_SIZE = 128
N_HEADS = 4
V_DIM = 128
QK_DIM = 64
ATTN_WIDTH = N_HEADS * V_DIM
POOL_WIDTH = 512
QK_SCALE = QK_DIM ** -0.5
IN_WIDTH = 4 * ATTN_WIDTH
POOL_WINDOWS = (2, 4, 8, 16)
POOL_GROUP = 128
POOL_STATE = 15
NUM_BUCKETS = 32
MAX_DISTANCE = 128
N_EXPERTS = 32
TOP_K = 4
D_FF = D_MODEL
SWIGLU_LIMIT = 7.0
SWIGLU_ALPHA = 1.702
DEPTH = 1
DN_ALPHA = (2.0 * DEPTH) ** 0.25
LN_EPS = 1e-5
LAMBDA_INIT = 0.8 - 0.6 * math.exp(-0.3 * 0)

LANES = 128
SUBLANES = 8
ROW_TILES = D_MODEL // LANES
NEG_BIG = -1e30
LOG2E = math.log2(math.e)
VMEM_LIMIT = 56 * 1024 * 1024
BF16_SUBLANES = 16
VT_ROWS = V_DIM + BF16_SUBLANES


def _ln(x, g, b):
    mu = jnp.mean(x, axis=-1, keepdims=True)
    xc = x - mu
    var = jnp.mean(xc * xc, axis=-1, keepdims=True)
    return xc * lax.rsqrt(var + LN_EPS) * g + b


def _lam(lp_ref):
    lp = lp_ref[...]
    s1 = jnp.sum(lp[0:1, :] * lp[1:2, :], axis=-1, keepdims=True)
    s2 = jnp.sum(lp[2:3, :] * lp[3:4, :], axis=-1, keepdims=True)
    return jnp.exp(s1) - jnp.exp(s2) + LAMBDA_INIT


def _to_row_tiles(ref, x):
    n = x.shape[0]
    for c in range(ROW_TILES):
        ref[pl.ds(c, n, stride=ROW_TILES), :] = x[:, c * LANES:(c + 1) * LANES]


def _from_row_tiles(ref, n, lead=()):
    return jnp.concatenate(
        [ref[lead + (pl.ds(c, n, stride=ROW_TILES), slice(None))] for c in range(ROW_TILES)], axis=1)


def _inproj_prompt_kernel(x_ref, g_ref, b_ref, w_ref,
                          k_ref, v_ref, u_ref, qt_ref, kb_ref, vt_ref):
    h = _ln(x_ref[...], g_ref[...], b_ref[...]).astype(BF16)
    p = jnp.dot(h, w_ref[...], preferred_element_type=F32)
    q = p[:, 0:512] * (QK_SCALE * LOG2E)
    k = p[:, 512:1024]
    v = p[:, 1024:1536]
    k_ref[...] = k
    v_ref[...] = v
    u_ref[...] = p[:, 1536:2048]
    kb_ref[...] = k.astype(BF16)
    qt_ref[...] = q.T.astype(BF16)
    vt = v.T.astype(BF16)
    ones = jnp.ones((BF16_SUBLANES, vt.shape[1]), BF16)
    for hd in range(N_HEADS):
        vt_ref[hd * VT_ROWS:hd * VT_ROWS + V_DIM, :] = vt[hd * V_DIM:(hd + 1) * V_DIM, :]
        vt_ref[hd * VT_ROWS + V_DIM:(hd + 1) * VT_ROWS, :] = ones


def _inproj_sample_kernel(x_ref, g_ref, b_ref, w_ref, q_ref, k_ref, v_ref, u_ref):
    h = _ln(x_ref[...], g_ref[...], b_ref[...]).astype(BF16)
    p = jnp.dot(h, w_ref[...], preferred_element_type=F32)
    q_ref[...] = p[:, 0:512] * (QK_SCALE * LOG2E)
    k_ref[...] = p[:, 512:1024]
    v_ref[...] = p[:, 1024:1536]
    u_ref[...] = p[:, 1536:2048]


def _inproj(x, g, b, w16, *, tm, prompt):
    t = x.shape[0]
    row = lambda i: (i, 0)
    col = lambda i: (0, i)
    const = lambda i: (0, 0)
    in_specs = [pl.BlockSpec((tm, D_MODEL), row),
                pl.BlockSpec((1, D_MODEL), const),
                pl.BlockSpec((1, D_MODEL), const),
                pl.BlockSpec((D_MODEL, IN_WIDTH), const)]
    f32_out = jax.ShapeDtypeStruct((t, 512), F32)
    if prompt:
        out_shape = [f32_out, f32_out, f32_out,
                     jax.ShapeDtypeStruct((512, t), BF16),
                     jax.ShapeDtypeStruct((t, 512), BF16),
                     jax.ShapeDtypeStruct((N_HEADS * VT_ROWS, t), BF16)]
        out_specs = [pl.BlockSpec((tm, 512), row)] * 3 + [
            pl.BlockSpec((512, tm), col), pl.BlockSpec((tm, 512), row),
            pl.BlockSpec((N_HEADS * VT_ROWS, tm), col)]
        body = _inproj_prompt_kernel
    else:
        out_shape = [f32_out] * 4
        out_specs = [pl.BlockSpec((tm, 512), row)] * 4
        body = _inproj_sample_kernel
    return pl.pallas_call(
        body, grid=(t // tm,), in_specs=in_specs, out_specs=out_specs,
        out_shape=out_shape,
        compiler_params=pltpu.CompilerParams(
            dimension_semantics=("arbitrary",), vmem_limit_bytes=VMEM_LIMIT),
        name="inproj_prompt" if prompt else "inproj_sample",
    )(x, g, b, w16)


def _bucket(dist):
    max_exact = NUM_BUCKETS // 2
    df = jnp.maximum(dist, 1).astype(F32)
    large = max_exact + (jnp.log(df / max_exact) / math.log(MAX_DISTANCE / max_exact)
                         * (NUM_BUCKETS - max_exact)).astype(I32)
    large = jnp.minimum(large, NUM_BUCKETS - 1)
    return jnp.where(dist < max_exact, dist, large)


def _bias_rel(table, dist):
    far = table[NUM_BUCKETS - 1].astype(F32)
    b = (jnp.take(table.astype(F32), _bucket(dist), axis=0) - far) * LOG2E
    return jnp.moveaxis(b, -1, 0)


def _prompt_attn_step(i, qi_tab, ki_tab, qt_ref, k_ref, vt_ref, wv_ref, lp_ref, g_ref,
                      o_ref, qm_s, acc_s, m_s, l_s, bias_s, tq):
    qi = qi_tab[i]
    ki = ki_tab[i]

    @pl.when(i == 0)
    def _bias():
        for t in range(2 * N_HEADS):
            x = jnp.broadcast_to(wv_ref[t:t + 1, :], (tq, 2 * tq))
            r = pltpu.roll(x, 0, 1, stride=1, stride_axis=0)
            bias_s[t] = r[:, tq:2 * tq]

    @pl.when(ki == 0)
    def _init():
        rowid = lax.broadcasted_iota(I32, (V_DIM, tq), 0)
        for h in range(N_HEADS):
            qh = qt_ref[h * 128:(h + 1) * 128, :].astype(F32)
            qm_s[2 * h] = jnp.where(rowid < QK_DIM, qh, 0.0).astype(BF16)
            qm_s[2 * h + 1] = jnp.where(rowid >= QK_DIM, qh, 0.0).astype(BF16)
        acc_s[...] = jnp.zeros_like(acc_s)
        l_s[...] = jnp.zeros_like(l_s)
        m_s[...] = jnp.full_like(m_s, NEG_BIG)

    def step(bias_base):
        def scores(j):
            h = j // 2
            s = jnp.dot(k_ref[:, h * 128:(h + 1) * 128], qm_s[j],
                        preferred_element_type=F32)
            if bias_base is not None:
                s = s + bias_s[bias_base + h]
            return s

        pending = [scores(0), scores(1)]
        for j in range(2 * N_HEADS):
            h = j // 2
            s = pending.pop(0)
            m_old = m_s[j:j + 1, :]
            m_new = jnp.maximum(m_old, jnp.max(s, axis=0, keepdims=True))
            alpha = jnp.exp2(m_old - m_new)
            p16 = jnp.exp2((s - m_new).astype(BF16))
            m_s[j:j + 1, :] = m_new
            if j + 2 < 2 * N_HEADS:
                pending.append(scores(j + 2))
            pv = jnp.dot(vt_ref[h * VT_ROWS:(h + 1) * VT_ROWS, :], p16,
                         preferred_element_type=F32)
            l_s[j:j + 1, :] = alpha * l_s[j:j + 1, :] + pv[V_DIM:V_DIM + 1, :]
            acc_s[j] = alpha * acc_s[j] + pv[0:V_DIM, :]

    @pl.when(ki < qi - 1)
    def _far():
        step(None)

    @pl.when(ki == qi - 1)
    def _sub():
        step(0)

    @pl.when(ki == qi)
    def _diag():
        step(N_HEADS)
        lam = _lam(lp_ref)
        for h in range(N_HEADS):
            o1 = acc_s[2 * h] * (1.0 / l_s[2 * h:2 * h + 1, :])
            o2 = acc_s[2 * h + 1] * (1.0 / l_s[2 * h + 1:2 * h + 2, :])
            a = o1 - lam * o2
            ms = jnp.mean(a * a, axis=0, keepdims=True)
            a = a * lax.rsqrt(ms + LN_EPS) * g_ref[...] * (1.0 - LAMBDA_INIT)
            o_ref[:, h * 128:(h + 1) * 128] = a.T.astype(BF16)


def _sample_page_copies(st, sl, pt_ref, kt_hbm, v_hbm, kbuf, vbuf, ksem, vsem, pg, n_chunks):
    bb = st // n_chunks
    cc = st % n_chunks
    cps = []
    for j in range(pg):
        page = pt_ref[bb, cc * pg + j]
        cps.append(pltpu.make_async_copy(kt_hbm.at[page], kbuf.at[sl, j], ksem.at[sl]))
        cps.append(pltpu.make_async_copy(v_hbm.at[page], vbuf.at[sl, j], vsem.at[sl]))
    return cps


def _sample_attn_step(step, q_ref, kn_ref, vn_ref, bl_ref, bn_ref, lp_ref, g_ref, o_ref,
                      kbuf, vbuf, acc_s, m_s, l_s, pg, n_chunks):
    c = step % n_chunks
    slot = step % 2
    last = c == n_chunks - 1

    @pl.when(c == 0)
    def _init():
        acc_s[...] = jnp.zeros_like(acc_s)
        l_s[...] = jnp.zeros_like(l_s)
        m_s[...] = jnp.full_like(m_s, NEG_BIG)

    qr = q_ref[0]
    kcat = jnp.concatenate([kbuf[slot, j].astype(BF16) for j in range(pg)], axis=1)
    s = jnp.dot(qr, kcat, preferred_element_type=F32)
    s = jnp.concatenate([s[:, :(pg - 1) * PAGE_SIZE],
                         s[:, (pg - 1) * PAGE_SIZE:] + jnp.where(last, bl_ref[...], 0.0)], axis=1)
    m_old = m_s[...]
    m_new = jnp.maximum(m_old, jnp.max(s, axis=1, keepdims=True))
    alpha = jnp.exp2(m_old - m_new)
    p = jnp.exp2(s - m_new)
    l_s[...] = alpha * l_s[...] + jnp.sum(p, axis=1, keepdims=True)
    vcat = jnp.concatenate(
        [jnp.concatenate([vbuf[slot, j, pl.ds(h, PAGE_SIZE, stride=N_HEADS), :]
                          for h in range(N_HEADS)], axis=1).astype(BF16)
         for j in range(pg)], axis=0)
    acc_s[...] = alpha * acc_s[...] + jnp.dot(p.astype(BF16), vcat, preferred_element_type=F32)
    m_s[...] = m_new

    @pl.when(last)
    def _fin():
        nt = (((1,), (1,)), ((), ()))
        kn = kn_ref[0].astype(BF16).astype(F32)
        vn = vn_ref[0].astype(BF16).astype(F32)
        s_new = lax.dot_general(qr.astype(F32), kn, nt,
                                preferred_element_type=F32) + bn_ref[...]
        m0 = m_s[...]
        m1 = jnp.maximum(m0, jnp.max(s_new, axis=1, keepdims=True))
        a1 = jnp.exp2(m0 - m1)
        p_new = jnp.exp2(s_new - m1)
        l1 = a1 * l_s[...] + jnp.sum(p_new, axis=1, keepdims=True)
        p_new = p_new.astype(BF16).astype(F32)
        acc = a1 * acc_s[...] + jnp.dot(p_new, vn, preferred_element_type=F32)
        o = acc * (1.0 / l1)
        lam = _lam(lp_ref)
        for h in range(N_HEADS):
            blk = o[h * 16:(h + 1) * 16, h * 128:(h + 1) * 128]
            a = blk[0:8, :] - lam * blk[8:16, :]
            ms = jnp.mean(a * a, axis=-1, keepdims=True)
            a = a * lax.rsqrt(ms + LN_EPS) * g_ref[...] * (1.0 - LAMBDA_INIT)
            o_ref[0, :, h * 128:(h + 1) * 128] = a


def _attn_kernel(qi_tab, ki_tab, pt_ref,
                 qt_ref, k_ref, vt_ref, wv_ref, lp_ref, gcol_ref,
                 qs_ref, kn_ref, vn_ref, bl_ref, bn_ref, grow_ref, kt_hbm, v_hbm,
                 op_ref, os_ref,
                 qm_s, acc_s, m_s, l_s, bias_s, kbuf, vbuf, ksem, vsem, sacc_s, sm_s, sl_s,
                 *, tq, pg, n_chunks, n_sample_steps):
    i = pl.program_id(0)
    slot = i % 2
    copies = functools.partial(_sample_page_copies, pt_ref=pt_ref, kt_hbm=kt_hbm, v_hbm=v_hbm,
                               kbuf=kbuf, vbuf=vbuf, ksem=ksem, vsem=vsem, pg=pg,
                               n_chunks=n_chunks)

    @pl.when(i == 0)
    def _prime():
        for cp in copies(0, 0):
            cp.start()

    @pl.when(i + 1 < n_sample_steps)
    def _prefetch():
        for cp in copies(i + 1, 1 - slot):
            cp.start()

    _prompt_attn_step(i, qi_tab, ki_tab, qt_ref, k_ref, vt_ref, wv_ref, lp_ref, gcol_ref,
                      op_ref, qm_s, acc_s, m_s, l_s, bias_s, tq)

    @pl.when(i < n_sample_steps)
    def _sample():
        for cp in copies(i, slot):
            cp.wait()
        _sample_attn_step(i, qs_ref, kn_ref, vn_ref, bl_ref, bn_ref, lp_ref, grow_ref, os_ref,
                          kbuf, vbuf, sacc_s, sm_s, sl_s, pg, n_chunks)


def _attention(qt, kb, vt, q_s, k_new, v_new, cache_k, cache_v, page_table, table, lam_params,
               subln_g, *, tq, pg):
    t_p = kb.shape[0]
    nq = t_p // tq
    pairs = [(a, b) for a in range(nq) for b in range(a + 1)]
    qi_tab = jnp.asarray([p[0] for p in pairs], I32)
    ki_tab = jnp.asarray([p[1] for p in pairs], I32)
    c = jnp.arange(2 * tq)
    w_sub = _bias_rel(table, c)
    w_diag = jnp.where(c >= tq, _bias_rel(table, jnp.maximum(c - tq, 0)), NEG_BIG)
    wvec = jnp.concatenate([w_sub, w_diag], axis=0)

    nb, n_pages = page_table.shape
    t = q_s.shape[0] // nb
    n_chunks = n_pages // pg
    ns = nb * n_chunks
    assert ns <= len(pairs), "sample chunks must fit in the prompt's grid steps"
    past = n_pages * PAGE_SIZE
    n_pool = cache_k.shape[0]
    kt = jnp.transpose(cache_k, (0, 2, 3, 4, 1)).reshape(n_pool, 512, PAGE_SIZE)
    v2 = cache_v.reshape(n_pool, PAGE_SIZE * N_HEADS, V_DIM)
    q5 = q_s.reshape(nb, t, 8, QK_DIM)
    eye = jnp.eye(8, dtype=F32)
    qrows = (jnp.transpose(q5, (0, 2, 1, 3))[:, :, :, None, :] * eye[None, :, None, :, None])
    qrows = qrows.reshape(nb, 8 * t, 512).astype(BF16)
    tt = jnp.arange(t)
    jj = jnp.arange(PAGE_SIZE)
    d_last = (past + tt[:, None]) - (past - PAGE_SIZE + jj[None, :])
    b_last = _bias_rel(table, d_last)
    b_last = jnp.broadcast_to(b_last[:, None], (N_HEADS, 2, t, PAGE_SIZE)).reshape(8 * t, PAGE_SIZE)
    d_new = tt[:, None] - tt[None, :]
    b_new = jnp.where(d_new >= 0, _bias_rel(table, jnp.maximum(d_new, 0)), NEG_BIG)
    b_new = jnp.broadcast_to(b_new[:, None], (N_HEADS, 2, t, t)).reshape(8 * t, t)

    const = lambda i, qt_, kt_, pt_: (0, 0)
    seq3 = lambda i, qt_, kt_, pt_: (jnp.minimum(i, ns - 1) // n_chunks, 0, 0)
    in_specs = [
        pl.BlockSpec((512, tq), lambda i, qt_, kt_, pt_: (0, qt_[i])),
        pl.BlockSpec((tq, 512), lambda i, qt_, kt_, pt_: (kt_[i], 0)),
        pl.BlockSpec((N_HEADS * VT_ROWS, tq), lambda i, qt_, kt_, pt_: (0, kt_[i])),
        pl.BlockSpec((2 * N_HEADS, 2 * tq), const),
        pl.BlockSpec((4, QK_DIM), const),
        pl.BlockSpec((V_DIM, 1), const),
        pl.BlockSpec((1, 8 * t, 512), seq3),
        pl.BlockSpec((1, t, 512), seq3), pl.BlockSpec((1, t, 512), seq3),
        pl.BlockSpec((8 * t, PAGE_SIZE), const),
        pl.BlockSpec((8 * t, t), const),
        pl.BlockSpec((1, V_DIM), const),
        pl.BlockSpec(memory_space=pl.ANY),
        pl.BlockSpec(memory_space=pl.ANY)]
    grid_spec = pltpu.PrefetchScalarGridSpec(
        num_scalar_prefetch=3, grid=(len(pairs),), in_specs=in_specs,
        out_specs=[pl.BlockSpec((tq, 512), lambda i, qt_, kt_, pt_: (qt_[i], 0)),
                   pl.BlockSpec((1, t, 512), seq3)],
        scratch_shapes=[pltpu.VMEM((8, V_DIM, tq), BF16),
                        pltpu.VMEM((8, V_DIM, tq), F32),
                        pltpu.VMEM((8, tq), F32),
                        pltpu.VMEM((8, tq), F32),
                        pltpu.VMEM((2 * N_HEADS, tq, tq), F32),
                        pltpu.VMEM((2, pg, 512, PAGE_SIZE), F32),
                        pltpu.VMEM((2, pg, PAGE_SIZE * N_HEADS, V_DIM), F32),
                        pltpu.SemaphoreType.DMA((2,)),
                        pltpu.SemaphoreType.DMA((2,)),
                        pltpu.VMEM((8 * t, 512), F32),
                        pltpu.VMEM((8 * t, 1), F32),
                        pltpu.VMEM((8 * t, 1), F32)])
    a_p, a_s = pl.pallas_call(
        functools.partial(_attn_kernel, tq=tq, pg=pg, n_chunks=n_chunks, n_sample_steps=ns),
        grid_spec=grid_spec,
        out_shape=[jax.ShapeDtypeStruct((t_p, 512), BF16),
                   jax.ShapeDtypeStruct((nb, t, 512), F32)],
        compiler_params=pltpu.CompilerParams(
            dimension_semantics=("arbitrary",), vmem_limit_bytes=VMEM_LIMIT),
        name="attention",
    )(qi_tab, ki_tab, page_table, qt, kb, vt, wvec, lam_params, subln_g.reshape(V_DIM, 1),
      qrows, k_new.reshape(nb, t, 512), v_new.reshape(nb, t, 512), b_last, b_new,
      subln_g.reshape(1, V_DIM), kt, v2)
    return a_p, a_s.reshape(nb * t, 512)


def _mix_tail(x, a16, pooled, u_now, wpool_ref, ps_ref, wout_ref, ging_ref, binb_ref,
              g1_ref, b1_ref, wr_ref, br_ref,
              h1_ref, h1t_ref, e_ref, gt_ref, rk_ref, cnt_ref, carry_s):
    i = pl.program_id(0)
    tm = x.shape[0]
    hp = _ln(x, ging_ref[...], binb_ref[...])
    pooled = pooled - u_now
    zs = [jnp.dot(pooled[:, g * 128:(g + 1) * 128].astype(BF16), wpool_ref[g],
                  preferred_element_type=F32) for g in range(4)]
    z = jnp.concatenate(zs, axis=1) * ps_ref[...]
    mix = (jnp.dot(a16, wout_ref[0:512, :], preferred_element_type=F32)
           + jnp.dot(z.astype(BF16), wout_ref[512:1024, :], preferred_element_type=F32))
    h1 = _ln(DN_ALPHA * hp + mix, g1_ref[...], b1_ref[...])
    h1_ref[...] = h1
    _to_row_tiles(h1t_ref, h1)

    x_hi = h1.astype(BF16)
    x_lo = (h1 - x_hi.astype(F32)).astype(BF16)
    w = wr_ref[...]
    w_hi = w.astype(BF16)
    w_lo = (w - w_hi.astype(F32)).astype(BF16)
    logits = (jnp.dot(x_hi, w_hi, preferred_element_type=F32)
              + jnp.dot(x_lo, w_hi, preferred_element_type=F32)
              + jnp.dot(x_hi, w_lo, preferred_element_type=F32)) + br_ref[...]
    lane = lax.broadcasted_iota(I32, (tm, LANES), 1).astype(F32)
    cur = jnp.where(lane < N_EXPERTS, logits, NEG_BIG)
    vals, idxs = [], []
    for _ in range(TOP_K):
        mx = jnp.max(cur, axis=1, keepdims=True)
        idx = jnp.min(jnp.where(cur == mx, lane, float(LANES)), axis=1, keepdims=True)
        vals.append(mx)
        idxs.append(idx)
        cur = jnp.where(lane == idx, NEG_BIG, cur)
    ex = [jnp.exp(v - vals[0]) for v in vals]
    den = ex[0] + ex[1] + ex[2] + ex[3]
    gates = [e / den for e in ex]

    @pl.when(i == 0)
    def _zero():
        carry_s[...] = jnp.zeros_like(carry_s)

    sel = [lane == idx for idx in idxs]
    member = jnp.zeros((tm, LANES), F32)
    for s_ in sel:
        member = jnp.where(s_, 1.0, member)
    r_id = lax.broadcasted_iota(I32, (tm, tm), 0)
    c_id = lax.broadcasted_iota(I32, (tm, tm), 1)
    lower = jnp.where(c_id < r_id, 1.0, 0.0).astype(BF16)
    pref = jnp.dot(lower, member.astype(BF16), preferred_element_type=F32) + carry_s[...]
    ranks = [jnp.sum(jnp.where(s_, pref, 0.0), axis=1, keepdims=True) for s_ in sel]
    carry_s[...] = carry_s[...] + jnp.sum(member, axis=0, keepdims=True)
    cnt_ref[...] = carry_s[...].astype(I32)

    e_out = jnp.zeros((tm, LANES), F32)
    g_out = jnp.zeros((tm, LANES), F32)
    r_out = jnp.zeros((tm, LANES), F32)
    for k in range(TOP_K):
        at_k = lane == float(k)
        e_out = jnp.where(at_k, idxs[k], e_out)
        g_out = jnp.where(at_k, gates[k], g_out)
        r_out = jnp.where(at_k, ranks[k], r_out)
    e_ref[...] = e_out.astype(I32)
    gt_ref[...] = g_out
    rk_ref[...] = r_out.astype(I32)


def _mix_prompt_kernel(x_ref, a_ref, u_ref, up_ref, *refs):
    ext_s = refs[-1]
    refs = refs[:-1]
    i = pl.program_id(0)
    tm = x_ref.shape[0]
    ext_s[0:16, :] = jnp.where(i == 0, 0.0, up_ref[...])
    u = u_ref[...]
    ext_s[16:16 + tm, :] = u
    pos = i * tm + lax.broadcasted_iota(I32, (tm, 1), 0)
    means = []
    for g, w in enumerate(POOL_WINDOWS):
        sl = slice(g * POOL_GROUP, (g + 1) * POOL_GROUP)
        s = u[:, sl]
        for j in range(1, w):
            s = s + ext_s[16 - j:16 - j + tm, sl]
        cnt = jnp.minimum(pos + 1, w).astype(F32)
        means.append(s / cnt)
    pooled = jnp.concatenate(means, axis=1)
    _mix_tail(x_ref[...], a_ref[...], pooled, u, *refs)


def _mix_sample_kernel(x_ref, a_ref, ue_ref, *refs):
    bs = ue_ref.shape[0]
    t = ue_ref.shape[1] - 16
    u = ue_ref[:, 16:16 + t, :]
    means = []
    for g, w in enumerate(POOL_WINDOWS):
        sl = slice(g * POOL_GROUP, (g + 1) * POOL_GROUP)
        s = u[:, :, sl]
        for j in range(1, w):
            s = s + ue_ref[:, 16 - j:16 - j + t, sl]
        means.append(s / float(w))
    pooled = jnp.concatenate(means, axis=2).reshape(bs * t, POOL_WIDTH)
    _mix_tail(x_ref[...], a_ref[...].astype(BF16), pooled, u.reshape(bs * t, POOL_WIDTH), *refs)


def _mix(x, a, u_or_ext, params, *, tm, prompt):
    t = x.shape[0]
    row = lambda i: (i, 0)
    const = lambda i: (0, 0)
    const3 = lambda i: (0, 0, 0)
    (wpool16, pscale, wout16, ln_in_g, ln_in_b, ln1_g, ln1_b, wr_pad, br_pad) = params
    param_specs = [pl.BlockSpec((4, POOL_GROUP, POOL_GROUP), const3),
                   pl.BlockSpec((1, POOL_WIDTH), const),
                   pl.BlockSpec((D_MODEL, D_MODEL), const),
                   pl.BlockSpec((1, D_MODEL), const), pl.BlockSpec((1, D_MODEL), const),
                   pl.BlockSpec((1, D_MODEL), const), pl.BlockSpec((1, D_MODEL), const),
                   pl.BlockSpec((D_MODEL, LANES), const), pl.BlockSpec((1, LANES), const)]
    if prompt:
        data = [x, a, u_or_ext, u_or_ext]
        data_specs = [pl.BlockSpec((tm, D_MODEL), row), pl.BlockSpec((tm, 512), row),
                      pl.BlockSpec((tm, 512), row),
                      pl.BlockSpec((16, 512), lambda i: (jnp.maximum(i * (tm // 16) - 1, 0), 0))]
        body = _mix_prompt_kernel
        scratch = [pltpu.VMEM((1, LANES), F32), pltpu.VMEM((16 + tm, 512), F32)]
    else:
        seq_t = u_or_ext.shape[1] - 16
        bs = tm // seq_t
        data = [x, a, u_or_ext]
        data_specs = [pl.BlockSpec((tm, D_MODEL), row), pl.BlockSpec((tm, 512), row),
                      pl.BlockSpec((bs, 16 + seq_t, 512), lambda i: (i, 0, 0))]
        body = _mix_sample_kernel
        scratch = [pltpu.VMEM((1, LANES), F32)]
    out_shape = [jax.ShapeDtypeStruct((t, D_MODEL), F32),
                 jax.ShapeDtypeStruct((t * ROW_TILES, LANES), F32),
                 jax.ShapeDtypeStruct((t, LANES), I32),
                 jax.ShapeDtypeStruct((t, LANES), F32),
                 jax.ShapeDtypeStruct((t, LANES), I32),
                 jax.ShapeDtypeStruct((1, LANES), I32)]
    out_specs = [pl.BlockSpec((tm, D_MODEL), row), pl.BlockSpec((tm * ROW_TILES, LANES), row),
                 pl.BlockSpec((tm, LANES), row),
                 pl.BlockSpec((tm, LANES), row), pl.BlockSpec((tm, LANES), row),
                 pl.BlockSpec((1, LANES), const)]
    return pl.pallas_call(
        body, grid=(t // tm,), in_specs=data_specs + param_specs, out_specs=out_specs,
        out_shape=out_shape, scratch_shapes=scratch,
        compiler_params=pltpu.CompilerParams(
            dimension_semantics=("arbitrary",), vmem_limit_bytes=VMEM_LIMIT),
        name="mix_prompt" if prompt else "mix_sample",
    )(*data, wpool16, pscale, wout16, ln_in_g, ln_in_b, ln1_g, ln1_b, wr_pad, br_pad)


def _tile_copy(src, src_row, dst, dst_row, sem):
    return pltpu.make_async_copy(
        src.at[pl.ds(pl.multiple_of(src_row * ROW_TILES, ROW_TILES), ROW_TILES)],
        dst.at[pl.ds(pl.multiple_of(dst_row * ROW_TILES, ROW_TILES), ROW_TILES)], sem)


def _experts_kernel(be_ref, nu_ref, tok0_ref, tokn_ref, dst_ref, x_hbm, wgu_ref, bgu_ref,
                    wd_ref, bd_ref, yc_hbm, wgu16_s, wd16_s, xbuf, ybuf, gsem, ssem, *, tm):
    i = pl.program_id(0)
    nu = nu_ref[0]
    slot = i % 2
    e = be_ref[i]
    prev = be_ref[jnp.maximum(i - 1, 0)]
    half = tm // 2

    def gather_start(tok_ref, sl):
        for r in range(tm):
            _tile_copy(x_hbm, tok_ref[0, 0, r], xbuf.at[sl], r, gsem.at[sl]).start()

    def gather_wait(sl):
        for r in range(tm):
            _tile_copy(x_hbm, 0, xbuf.at[sl], r, gsem.at[sl]).wait()

    def scatter(sl, wait):
        for r in range(tm):
            cp = _tile_copy(ybuf.at[sl], r, yc_hbm, dst_ref[0, 0, r], ssem)
            cp.wait() if wait else cp.start()

    @pl.when(i == 0)
    def _prime():
        gather_start(tok0_ref, 0)
        ybuf[1] = jnp.zeros(ybuf.shape[1:], F32)

    @pl.when((i == 0) | (e != prev))
    def _cast():
        wgu16_s[...] = wgu_ref[0].astype(BF16)
        wd16_s[...] = wd_ref[0].astype(BF16)

    @pl.when(i < nu)
    def _run():
        gather_wait(slot)
        gather_start(tokn_ref, 1 - slot)
        scatter(1 - slot, wait=False)
        x = _from_row_tiles(xbuf.at[slot], tm).astype(BF16)
        gus = [jnp.dot(x[hh * half:(hh + 1) * half], wgu16_s[...],
                       preferred_element_type=F32) + bgu_ref[0] for hh in range(2)]
        for hh in range(2):
            gate = jnp.minimum(gus[hh][:, 0:D_FF], SWIGLU_LIMIT)
            up = jnp.clip(gus[hh][:, D_FF:2 * D_FF], -SWIGLU_LIMIT, SWIGLU_LIMIT)
            hdn = (up + 1.0) * (gate * jax.nn.sigmoid(SWIGLU_ALPHA * gate))
            y = jnp.dot(hdn.astype(BF16), wd16_s[...], preferred_element_type=F32) + bd_ref[0]
            for c in range(ROW_TILES):
                ybuf[slot, pl.ds(hh * half * ROW_TILES + c, half, stride=ROW_TILES), :] = (
                    y[:, c * LANES:(c + 1) * LANES])
        scatter(1 - slot, wait=True)

    @pl.when(i == nu)
    def _flush():
        gather_wait(slot)
        scatter(1 - slot, wait=False)
        scatter(1 - slot, wait=True)


def _moe_experts(h1t, row_tok, row_dst_prev, block_e, n_used, w_gu, b_gu, w_down, b_down,
                 *, tm, n_out_rows):
    rows = row_tok.shape[0]
    nblk = rows // tm
    tok3 = row_tok.reshape(nblk, 1, tm)
    smem_blk = lambda f: pl.BlockSpec((1, 1, tm), f, memory_space=pltpu.SMEM)
    grid_spec = pltpu.PrefetchScalarGridSpec(
        num_scalar_prefetch=2, grid=(nblk,),
        in_specs=[smem_blk(lambda i, be, nu: (0, 0, 0)),
                  smem_blk(lambda i, be, nu: (jnp.minimum(i + 1, nblk - 1), 0, 0)),
                  smem_blk(lambda i, be, nu: (i, 0, 0)),
                  pl.BlockSpec(memory_space=pl.ANY),
                  pl.BlockSpec((1, D_MODEL, 2 * D_FF), lambda i, be, nu: (be[i], 0, 0)),
                  pl.BlockSpec((1, 1, 2 * D_FF), lambda i, be, nu: (be[i], 0, 0)),
                  pl.BlockSpec((1, D_FF, D_MODEL), lambda i, be, nu: (be[i], 0, 0)),
                  pl.BlockSpec((1, 1, D_MODEL), lambda i, be, nu: (be[i], 0, 0))],
        out_specs=pl.BlockSpec(memory_space=pl.ANY),
        scratch_shapes=[pltpu.VMEM((D_MODEL, 2 * D_FF), BF16),
                        pltpu.VMEM((D_FF, D_MODEL), BF16),
                        pltpu.VMEM((2, tm * ROW_TILES, LANES), F32),
                        pltpu.VMEM((2, tm * ROW_TILES, LANES), F32),
                        pltpu.SemaphoreType.DMA((2,)),
                        pltpu.SemaphoreType.DMA])
    return pl.pallas_call(
        functools.partial(_experts_kernel, tm=tm), grid_spec=grid_spec,
        out_shape=jax.ShapeDtypeStruct((n_out_rows * ROW_TILES, LANES), F32),
        compiler_params=pltpu.CompilerParams(
            dimension_semantics=("arbitrary",), vmem_limit_bytes=VMEM_LIMIT),
        name="moe_experts",
    )(block_e, n_used, tok3, tok3, row_dst_prev.reshape(nblk, 1, tm), h1t,
      w_gu, b_gu.reshape(N_EXPERTS, 1, 2 * D_FF), w_down, b_down.reshape(N_EXPERTS, 1, D_MODEL))


def _finish_kernel(y0_ref, y1_ref, y2_ref, y3_ref, h1_ref, gt_ref, g2_ref, b2_ref, o_ref, *, tm):
    gt = gt_ref[...]
    f = _from_row_tiles(y0_ref, tm) * gt[:, 0:1]
    for k, y_ref in enumerate((y1_ref, y2_ref, y3_ref), start=1):
        f = f + _from_row_tiles(y_ref, tm) * gt[:, k:k + 1]
    o_ref[...] = _ln(DN_ALPHA * h1_ref[...] + f, g2_ref[...], b2_ref[...])


def _moe_finish(yc, h1, gates, ln2_g, ln2_b, *, tm):
    t = h1.shape[0]
    nblk = t // tm
    row = lambda i: (i, 0)
    const = lambda i: (0, 0)
    slot_spec = lambda k: pl.BlockSpec((tm * ROW_TILES, LANES), lambda i: (k * nblk + i, 0))
    return pl.pallas_call(
        functools.partial(_finish_kernel, tm=tm), grid=(nblk,),
        in_specs=[slot_spec(0), slot_spec(1), slot_spec(2), slot_spec(3),
                  pl.BlockSpec((tm, D_MODEL), row),
                  pl.BlockSpec((tm, LANES), row),
                  pl.BlockSpec((1, D_MODEL), const), pl.BlockSpec((1, D_MODEL), const)],
        out_specs=pl.BlockSpec((tm, D_MODEL), row),
        out_shape=jax.ShapeDtypeStruct((t, D_MODEL), F32),
        compiler_params=pltpu.CompilerParams(
            dimension_semantics=("arbitrary",), vmem_limit_bytes=VMEM_LIMIT),
        name="moe_finish",
    )(yc, yc, yc, yc, h1, gates, ln2_g, ln2_b)


def _moe(h1, h1t, top_e, gates, rank, counts, w_gu, b_gu, w_down, b_down, ln2_g, ln2_b,
         *, tm_e, tm_c):
    t = h1.shape[0]
    counts = counts[0, :N_EXPERTS]
    padded = (counts + tm_e - 1) // tm_e * tm_e
    pad_end = jnp.cumsum(padded)
    pad_start = pad_end - padded
    e4 = top_e[:, :TOP_K]
    onehot = e4[:, :, None] == jnp.arange(N_EXPERTS, dtype=I32)[None, None, :]
    dest = rank[:, :TOP_K] + jnp.sum(jnp.where(onehot, pad_start[None, None, :], 0), axis=-1)
    nblk = (t * TOP_K) // tm_e + N_EXPERTS
    rows = nblk * tm_e
    starts = jnp.arange(nblk, dtype=I32) * tm_e
    block_e = jnp.minimum(jnp.sum((pad_end[None, :] <= starts[:, None]).astype(I32), axis=1),
                          N_EXPERTS - 1).astype(I32)
    nu_e = (pad_end[-1] // tm_e).astype(I32).reshape(1)
    flat = jnp.arange(t * TOP_K, dtype=I32)
    code = (flat % TOP_K) * t + flat // TOP_K
    inv = jnp.full((rows,), -1, I32).at[dest.reshape(-1)].set(code, unique_indices=True)
    spare = TOP_K * t + jnp.arange(rows, dtype=I32) % tm_e
    row_tok = jnp.where(inv < 0, 0, inv % t)
    row_dst = jnp.where(inv < 0, spare, inv)
    row_dst_prev = jnp.concatenate([spare[:tm_e], row_dst[:rows - tm_e]])
    yc = _moe_experts(h1t, row_tok, row_dst_prev, block_e, nu_e, w_gu, b_gu, w_down, b_down,
                      tm=tm_e, n_out_rows=TOP_K * t + tm_e)
    return _moe_finish(yc, h1, gates, ln2_g, ln2_b, tm=tm_c)


def kernel(x_prompt, x_sample, cache_k, cache_v, state_pool, page_table, ln_in_g, ln_in_b, w_in, lambda_q1, lambda_k1, lambda_q2, lambda_k2, subln_g, rel_bias_table, w_pool, pool_scale, w_out, ln1_g, ln1_b, w_router, b_router, w_gate_up, b_gate_up, w_down, b_down, ln2_g, ln2_b):
    l = 0
    bsz, seq, _ = x_prompt.shape
    nb, dec_t, _ = x_sample.shape
    assert bsz == 1
    r2 = lambda a: a.reshape(1, -1)
    w_in16 = w_in[l].astype(BF16)
    lam_params = jnp.stack([lambda_q1[l], lambda_k1[l], lambda_q2[l], lambda_k2[l]]).astype(F32)
    wr_pad = jnp.pad(w_router[l], ((0, 0), (0, LANES - N_EXPERTS)))
    br_pad = jnp.pad(b_router[l], (0, LANES - N_EXPERTS)).reshape(1, LANES)
    mix_params = (w_pool[l].astype(BF16), r2(pool_scale[l]), w_out[l].astype(BF16),
                  r2(ln_in_g), r2(ln_in_b), r2(ln1_g[l]), r2(ln1_b[l]), wr_pad, br_pad)

    xp = x_prompt.reshape(seq, D_MODEL)
    xs = x_sample.reshape(nb * dec_t, D_MODEL)
    k_p, v_p, u_p, qt_p, kb_p, vt_p = _inproj(xp, r2(ln_in_g), r2(ln_in_b), w_in16,
                                              tm=512, prompt=True)
    q_s, k_s, v_s, u_s = _inproj(xs, r2(ln_in_g), r2(ln_in_b), w_in16, tm=512, prompt=False)
    a_p, a_s = _attention(qt_p, kb_p, vt_p, q_s, k_s, v_s, cache_k[l], cache_v[l], page_table,
                          rel_bias_table, lam_params, subln_g[l], tq=512, pg=16)

    h1_p, h1t_p, e_p, g_p, rk_p, cnt_p = _mix(xp, a_p, u_p, mix_params, tm=512, prompt=True)
    y_p = _moe(h1_p, h1t_p, e_p, g_p, rk_p, cnt_p, w_gate_up[l], b_gate_up[l], w_down[l],
               b_down[l], r2(ln2_g[l]), r2(ln2_b[l]), tm_e=256, tm_c=256)

    u_ext = jnp.concatenate([jnp.zeros((nb, 1, POOL_WIDTH), F32), state_pool[l],
                             u_s.reshape(nb, dec_t, POOL_WIDTH)], axis=1)
    h1_s, h1t_s, e_s, g_s, rk_s, cnt_s = _mix(xs, a_s, u_ext, mix_params, tm=256, prompt=False)
    y_s = _moe(h1_s, h1t_s, e_s, g_s, rk_s, cnt_s, w_gate_up[l], b_gate_up[l], w_down[l],
               b_down[l], r2(ln2_g[l]), r2(ln2_b[l]), tm_e=128, tm_c=256)

    k_prompt = k_p.reshape(1, bsz, seq, N_HEADS, 2, QK_DIM)
    v_prompt = v_p.reshape(1, bsz, seq, N_HEADS, V_DIM)
    pool_prompt = u_p[seq - POOL_STATE:].reshape(1, bsz, POOL_STATE, POOL_WIDTH)
    k_sample = k_s.reshape(1, nb, dec_t, N_HEADS, 2, QK_DIM)
    v_sample = v_s.reshape(1, nb, dec_t, N_HEADS, V_DIM)
    pool_sample = u_ext[:, 16 + dec_t - POOL_STATE:].reshape(1, nb, POOL_STATE, POOL_WIDTH)
    return (y_p.reshape(bsz, seq, D_MODEL), y_s.reshape(nb, dec_t, D_MODEL),
            k_prompt, v_prompt, pool_prompt, k_sample, v_sample, pool_sample)
```

```python
import functools
import math

import jax
import jax.numpy as jnp
from jax import lax
from jax.experimental import pallas as pl
from jax.experimental.pallas import tpu as pltpu

F32 = jnp.float32
BF16 = jnp.bfloat16
I32 = jnp.int32

D_MODEL = 1024
PAGE_SIZE = 128
N_HEADS = 4
V_DIM = 128
QK_DIM = 64
ATTN_WIDTH = N_HEADS * V_DIM
POOL_WIDTH = 512
QK_SCALE = QK_DIM ** -0.5
IN_WIDTH = 4 * ATTN_WIDTH
POOL_WINDOWS = (2, 4, 8, 16)
POOL_GROUP = 128
POOL_STATE = 15
NUM_BUCKETS = 32
MAX_DISTANCE = 128
N_EXPERTS = 32
TOP_K = 4
D_FF = D_MODEL
SWIGLU_LIMIT = 7.0
SWIGLU_ALPHA = 1.702
DEPTH = 1
DN_ALPHA = (2.0 * DEPTH) ** 0.25
LN_EPS = 1e-5
LAMBDA_INIT = 0.8 - 0.6 * math.exp(-0.3 * 0)

LANES = 128
SUBLANES = 8
ROW_TILES = D_MODEL // LANES
NEG_BIG = -1e30
LOG2E = math.log2(math.e)
VMEM_LIMIT = 56 * 1024 * 1024
BF16_SUBLANES = 16
VT_ROWS = V_DIM + BF16_SUBLANES


def _ln(x, g, b):
    mu = jnp.mean(x, axis=-1, keepdims=True)
    xc = x - mu
    var = jnp.mean(xc * xc, axis=-1, keepdims=True)
    return xc * lax.rsqrt(var + LN_EPS) * g + b


def _lam(lp_ref):
    lp = lp_ref[...]
    s1 = jnp.sum(lp[0:1, :] * lp[1:2, :], axis=-1, keepdims=True)
    s2 = jnp.sum(lp[2:3, :] * lp[3:4, :], axis=-1, keepdims=True)
    return jnp.exp(s1) - jnp.exp(s2) + LAMBDA_INIT


def _to_row_tiles(ref, x):
    n = x.shape[0]
    for c in range(ROW_TILES):
        ref[pl.ds(c, n, stride=ROW_TILES), :] = x[:, c * LANES:(c + 1) * LANES]


def _from_row_tiles(ref, n, lead=()):
    return jnp.concatenate(
        [ref[lead + (pl.ds(c, n, stride=ROW_TILES), slice(None))] for c in range(ROW_TILES)], axis=1)


def _inproj_prompt_kernel(x_ref, g_ref, b_ref, w_ref,
                          k_ref, v_ref, u_ref, qt_ref, kb_ref, vt_ref):
    h = _ln(x_ref[...], g_ref[...], b_ref[...]).astype(BF16)
    p = jnp.dot(h, w_ref[...], preferred_element_type=F32)
    q = p[:, 0:512] * (QK_SCALE * LOG2E)
    k = p[:, 512:1024]
    v = p[:, 1024:1536]
    k_ref[...] = k
    v_ref[...] = v
    u_ref[...] = p[:, 1536:2048]
    kb_ref[...] = k.astype(BF16)
    qt_ref[...] = q.T.astype(BF16)
    vt = v.T.astype(BF16)
    ones = jnp.ones((BF16_SUBLANES, vt.shape[1]), BF16)
    for hd in range(N_HEADS):
        vt_ref[hd * VT_ROWS:hd * VT_ROWS + V_DIM, :] = vt[hd * V_DIM:(hd + 1) * V_DIM, :]
        vt_ref[hd * VT_ROWS + V_DIM:(hd + 1) * VT_ROWS, :] = ones


def _inproj_sample_kernel(x_ref, g_ref, b_ref, w_ref, q_ref, k_ref, v_ref, u_ref):
    h = _ln(x_ref[...], g_ref[...], b_ref[...]).astype(BF16)
    p = jnp.dot(h, w_ref[...], preferred_element_type=F32)
    q_ref[...] = p[:, 0:512] * (QK_SCALE * LOG2E)
    k_ref[...] = p[:, 512:1024]
    v_ref[...] = p[:, 1024:1536]
    u_ref[...] = p[:, 1536:2048]


def _inproj(x, g, b, w16, *, tm, prompt):
    t = x.shape[0]
    row = lambda i: (i, 0)
    col = lambda i: (0, i)
    const = lambda i: (0, 0)
    in_specs = [pl.BlockSpec((tm, D_MODEL), row),
                pl.BlockSpec((1, D_MODEL), const),
                pl.BlockSpec((1, D_MODEL), const),
                pl.BlockSpec((D_MODEL, IN_WIDTH), const)]
    f32_out = jax.ShapeDtypeStruct((t, 512), F32)
    if prompt:
        out_shape = [f32_out, f32_out, f32_out,
                     jax.ShapeDtypeStruct((512, t), BF16),
                     jax.ShapeDtypeStruct((t, 512), BF16),
                     jax.ShapeDtypeStruct((N_HEADS * VT_ROWS, t), BF16)]
        out_specs = [pl.BlockSpec((tm, 512), row)] * 3 + [
            pl.BlockSpec((512, tm), col), pl.BlockSpec((tm, 512), row),
            pl.BlockSpec((N_HEADS * VT_ROWS, tm), col)]
        body = _inproj_prompt_kernel
    else:
        out_shape = [f32_out] * 4
        out_specs = [pl.BlockSpec((tm, 512), row)] * 4
        body = _inproj_sample_kernel
    return pl.pallas_call(
        body, grid=(t // tm,), in_specs=in_specs, out_specs=out_specs,
        out_shape=out_shape,
        compiler_params=pltpu.CompilerParams(
            dimension_semantics=("arbitrary",), vmem_limit_bytes=VMEM_LIMIT),
        name="inproj_prompt" if prompt else "inproj_sample",
    )(x, g, b, w16)


def _bucket(dist):
    max_exact = NUM_BUCKETS // 2
    df = jnp.maximum(dist, 1).astype(F32)
    large = max_exact + (jnp.log(df / max_exact) / math.log(MAX_DISTANCE / max_exact)
                         * (NUM_BUCKETS - max_exact)).astype(I32)
    large = jnp.minimum(large, NUM_BUCKETS - 1)
    return jnp.where(dist < max_exact, dist, large)


def _bias_rel(table, dist):
    far = table[NUM_BUCKETS - 1].astype(F32)
    b = (jnp.take(table.astype(F32), _bucket(dist), axis=0) - far) * LOG2E
    return jnp.moveaxis(b, -1, 0)


def _prompt_attn_step(i, qi_tab, ki_tab, qt_ref, k_ref, vt_ref, wv_ref, lp_ref, g_ref,
                      o_ref, qm_s, acc_s, m_s, l_s, bias_s, tq):
    qi = qi_tab[i]
    ki = ki_tab[i]

    @pl.when(i == 0)
    def _bias():
        for t in range(2 * N_HEADS):
            x = jnp.broadcast_to(wv_ref[t:t + 1, :], (tq, 2 * tq))
            r = pltpu.roll(x, 0, 1, stride=1, stride_axis=0)
            bias_s[t] = r[:, tq:2 * tq]

    @pl.when(ki == 0)
    def _init():
        rowid = lax.broadcasted_iota(I32, (V_DIM, tq), 0)
        for h in range(N_HEADS):
            qh = qt_ref[h * 128:(h + 1) * 128, :].astype(F32)
            qm_s[2 * h] = jnp.where(rowid < QK_DIM, qh, 0.0).astype(BF16)
            qm_s[2 * h + 1] = jnp.where(rowid >= QK_DIM, qh, 0.0).astype(BF16)
        acc_s[...] = jnp.zeros_like(acc_s)
        l_s[...] = jnp.zeros_like(l_s)
        m_s[...] = jnp.full_like(m_s, NEG_BIG)

    def step(bias_base):
        def scores(j):
            h = j // 2
            s = jnp.dot(k_ref[:, h * 128:(h + 1) * 128], qm_s[j],
                        preferred_element_type=F32)
            if bias_base is not None:
                s = s + bias_s[bias_base + h]
            return s

        pending = [scores(0), scores(1)]
        for j in range(2 * N_HEADS):
            h = j // 2
            s = pending.pop(0)
            m_old = m_s[j:j + 1, :]
            m_new = jnp.maximum(m_old, jnp.max(s, axis=0, keepdims=True))
            alpha = jnp.exp2(m_old - m_new)
            p16 = jnp.exp2((s - m_new).astype(BF16))
            m_s[j:j + 1, :] = m_new
            if j + 2 < 2 * N_HEADS:
                pending.append(scores(j + 2))
            pv = jnp.dot(vt_ref[h * VT_ROWS:(h + 1) * VT_ROWS, :], p16,
                         preferred_element_type=F32)
            l_s[j:j + 1, :] = alpha * l_s[j:j + 1, :] + pv[V_DIM:V_DIM + 1, :]
            acc_s[j] = alpha * acc_s[j] + pv[0:V_DIM, :]

    @pl.when(ki < qi - 1)
    def _far():
        step(None)

    @pl.when(ki == qi - 1)
    def _sub():
        step(0)

    @pl.when(ki == qi)
    def _diag():
        step(N_HEADS)
        lam = _lam(lp_ref)
        for h in range(N_HEADS):
            o1 = acc_s[2 * h] * (1.0 / l_s[2 * h:2 * h + 1, :])
            o2 = acc_s[2 * h + 1] * (1.0 / l_s[2 * h + 1:2 * h + 2, :])
            a = o1 - lam * o2
            ms = jnp.mean(a * a, axis=0, keepdims=True)
            a = a * lax.rsqrt(ms + LN_EPS) * g_ref[...] * (1.0 - LAMBDA_INIT)
            o_ref[:, h * 128:(h + 1) * 128] = a.T.astype(BF16)


def _sample_page_copies(st, sl, pt_ref, kt_hbm, v_hbm, kbuf, vbuf, ksem, vsem, pg, n_chunks):
    bb = st // n_chunks
    cc = st % n_chunks
    cps = []
    for j in range(pg):
        page = pt_ref[bb, cc * pg + j]
        cps.append(pltpu.make_async_copy(kt_hbm.at[page], kbuf.at[sl, j], ksem.at[sl]))
        cps.append(pltpu.make_async_copy(v_hbm.at[page], vbuf.at[sl, j], vsem.at[sl]))
    return cps


def _sample_attn_step(step, q_ref, kn_ref, vn_ref, bl_ref, bn_ref, lp_ref, g_ref, o_ref,
                      kbuf, vbuf, acc_s, m_s, l_s, pg, n_chunks):
    c = step % n_chunks
    slot = step % 2
    last = c == n_chunks - 1

    @pl.when(c == 0)
    def _init():
        acc_s[...] = jnp.zeros_like(acc_s)
        l_s[...] = jnp.zeros_like(l_s)
        m_s[...] = jnp.full_like(m_s, NEG_BIG)

    qr = q_ref[0]
    kcat = jnp.concatenate([kbuf[slot, j].astype(BF16) for j in range(pg)], axis=1)
    s = jnp.dot(qr, kcat, preferred_element_type=F32)
    s = jnp.concatenate([s[:, :(pg - 1) * PAGE_SIZE],
                         s[:, (pg - 1) * PAGE_SIZE:] + jnp.where(last, bl_ref[...], 0.0)], axis=1)
    m_old = m_s[...]
    m_new = jnp.maximum(m_old, jnp.max(s, axis=1, keepdims=True))
    alpha = jnp.exp2(m_old - m_new)
    p = jnp.exp2(s - m_new)
    l_s[...] = alpha * l_s[...] + jnp.sum(p, axis=1, keepdims=True)
    vcat = jnp.concatenate(
        [jnp.concatenate([vbuf[slot, j, pl.ds(h, PAGE_SIZE, stride=N_HEADS), :]
                          for h in range(N_HEADS)], axis=1).astype(BF16)
         for j in range(pg)], axis=0)
    acc_s[...] = alpha * acc_s[...] + jnp.dot(p.astype(BF16), vcat, preferred_element_type=F32)
    m_s[...] = m_new

    @pl.when(last)
    def _fin():
        nt = (((1,), (1,)), ((), ()))
        kn = kn_ref[0].astype(BF16).astype(F32)
        vn = vn_ref[0].astype(BF16).astype(F32)
        s_new = lax.dot_general(qr.astype(F32), kn, nt,
                                preferred_element_type=F32) + bn_ref[...]
        m0 = m_s[...]
        m1 = jnp.maximum(m0, jnp.max(s_new, axis=1, keepdims=True))
        a1 = jnp.exp2(m0 - m1)
        p_new = jnp.exp2(s_new - m1)
        l1 = a1 * l_s[...] + jnp.sum(p_new, axis=1, keepdims=True)
        p_new = p_new.astype(BF16).astype(F32)
        acc = a1 * acc_s[...] + jnp.dot(p_new, vn, preferred_element_type=F32)
        o = acc * (1.0 / l1)
        lam = _lam(lp_ref)
        for h in range(N_HEADS):
            blk = o[h * 16:(h + 1) * 16, h * 128:(h + 1) * 128]
            a = blk[0:8, :] - lam * blk[8:16, :]
            ms = jnp.mean(a * a, axis=-1, keepdims=True)
            a = a * lax.rsqrt(ms + LN_EPS) * g_ref[...] * (1.0 - LAMBDA_INIT)
            o_ref[0, :, h * 128:(h + 1) * 128] = a


def _attn_kernel(qi_tab, ki_tab, pt_ref,
                 qt_ref, k_ref, vt_ref, wv_ref, lp_ref, gcol_ref,
                 qs_ref, kn_ref, vn_ref, bl_ref, bn_ref, grow_ref, kt_hbm, v_hbm,
                 op_ref, os_ref,
                 qm_s, acc_s, m_s, l_s, bias_s, kbuf, vbuf, ksem, vsem, sacc_s, sm_s, sl_s,
                 *, tq, pg, n_chunks, n_sample_steps):
    i = pl.program_id(0)
    slot = i % 2
    copies = functools.partial(_sample_page_copies, pt_ref=pt_ref, kt_hbm=kt_hbm, v_hbm=v_hbm,
                               kbuf=kbuf, vbuf=vbuf, ksem=ksem, vsem=vsem, pg=pg,
                               n_chunks=n_chunks)

    @pl.when(i == 0)
    def _prime():
        for cp in copies(0, 0):
            cp.start()

    @pl.when(i + 1 < n_sample_steps)
    def _prefetch():
        for cp in copies(i + 1, 1 - slot):
            cp.start()

    _prompt_attn_step(i, qi_tab, ki_tab, qt_ref, k_ref, vt_ref, wv_ref, lp_ref, gcol_ref,
                      op_ref, qm_s, acc_s, m_s, l_s, bias_s, tq)

    @pl.when(i < n_sample_steps)
    def _sample():
        for cp in copies(i, slot):
            cp.wait()
        _sample_attn_step(i, qs_ref, kn_ref, vn_ref, bl_ref, bn_ref, lp_ref, grow_ref, os_ref,
                          kbuf, vbuf, sacc_s, sm_s, sl_s, pg, n_chunks)


def _attention(qt, kb, vt, q_s, k_new, v_new, cache_k, cache_v, page_table, table, lam_params,
               subln_g, *, tq, pg):
    t_p = kb.shape[0]
    nq = t_p // tq
    pairs = [(a, b) for a in range(nq) for b in range(a + 1)]
    qi_tab = jnp.asarray([p[0] for p in pairs], I32)
    ki_tab = jnp.asarray([p[1] for p in pairs], I32)
    c = jnp.arange(2 * tq)
    w_sub = _bias_rel(table, c)
    w_diag = jnp.where(c >= tq, _bias_rel(table, jnp.maximum(c - tq, 0)), NEG_BIG)
    wvec = jnp.concatenate([w_sub, w_diag], axis=0)

    nb, n_pages = page_table.shape
    t = q_s.shape[0] // nb
    n_chunks = n_pages // pg
    ns = nb * n_chunks
    assert ns <= len(pairs), "sample chunks must fit in the prompt's grid steps"
    past = n_pages * PAGE_SIZE
    n_pool = cache_k.shape[0]
    kt = jnp.transpose(cache_k, (0, 2, 3, 4, 1)).reshape(n_pool, 512, PAGE_SIZE)
    v2 = cache_v.reshape(n_pool, PAGE_SIZE * N_HEADS, V_DIM)
    q5 = q_s.reshape(nb, t, 8, QK_DIM)
    eye = jnp.eye(8, dtype=F32)
    qrows = (jnp.transpose(q5, (0, 2, 1, 3))[:, :, :, None, :] * eye[None, :, None, :, None])
    qrows = qrows.reshape(nb, 8 * t, 512).astype(BF16)
    tt = jnp.arange(t)
    jj = jnp.arange(PAGE_SIZE)
    d_last = (past + tt[:, None]) - (past - PAGE_SIZE + jj[None, :])
    b_last = _bias_rel(table, d_last)
    b_last = jnp.broadcast_to(b_last[:, None], (N_HEADS, 2, t, PAGE_SIZE)).reshape(8 * t, PAGE_SIZE)
    d_new = tt[:, None] - tt[None, :]
    b_new = jnp.where(d_new >= 0, _bias_rel(table, jnp.maximum(d_new, 0)), NEG_BIG)
    b_new = jnp.broadcast_to(b_new[:, None], (N_HEADS, 2, t, t)).reshape(8 * t, t)

    const = lambda i, qt_, kt_, pt_: (0, 0)
    seq3 = lambda i, qt_, kt_, pt_: (jnp.minimum(i, ns - 1) // n_chunks, 0, 0)
    in_specs = [
        pl.BlockSpec((512, tq), lambda i, qt_, kt_, pt_: (0, qt_[i])),
        pl.BlockSpec((tq, 512), lambda i, qt_, kt_, pt_: (kt_[i], 0)),
        pl.BlockSpec((N_HEADS * VT_ROWS, tq), lambda i, qt_, kt_, pt_: (0, kt_[i])),
        pl.BlockSpec((2 * N_HEADS, 2 * tq), const),
        pl.BlockSpec((4, QK_DIM), const),
        pl.BlockSpec((V_DIM, 1), const),
        pl.BlockSpec((1, 8 * t, 512), seq3),
        pl.BlockSpec((1, t, 512), seq3), pl.BlockSpec((1, t, 512), seq3),
        pl.BlockSpec((8 * t, PAGE_SIZE), const),
        pl.BlockSpec((8 * t, t), const),
        pl.BlockSpec((1, V_DIM), const),
        pl.BlockSpec(memory_space=pl.ANY),
        pl.BlockSpec(memory_space=pl.ANY)]
    grid_spec = pltpu.PrefetchScalarGridSpec(
        num_scalar_prefetch=3, grid=(len(pairs),), in_specs=in_specs,
        out_specs=[pl.BlockSpec((tq, 512), lambda i, qt_, kt_, pt_: (qt_[i], 0)),
                   pl.BlockSpec((1, t, 512), seq3)],
        scratch_shapes=[pltpu.VMEM((8, V_DIM, tq), BF16),
                        pltpu.VMEM((8, V_DIM, tq), F32),
                        pltpu.VMEM((8, tq), F32),
                        pltpu.VMEM((8, tq), F32),
                        pltpu.VMEM((2 * N_HEADS, tq, tq), F32),
                        pltpu.VMEM((2, pg, 512, PAGE_SIZE), F32),
                        pltpu.VMEM((2, pg, PAGE_SIZE * N_HEADS, V_DIM), F32),
                        pltpu.SemaphoreType.DMA((2,)),
                        pltpu.SemaphoreType.DMA((2,)),
                        pltpu.VMEM((8 * t, 512), F32),
                        pltpu.VMEM((8 * t, 1), F32),
                        pltpu.VMEM((8 * t, 1), F32)])
    a_p, a_s = pl.pallas_call(
        functools.partial(_attn_kernel, tq=tq, pg=pg, n_chunks=n_chunks, n_sample_steps=ns),
        grid_spec=grid_spec,
        out_shape=[jax.ShapeDtypeStruct((t_p, 512), BF16),
                   jax.ShapeDtypeStruct((nb, t, 512), F32)],
        compiler_params=pltpu.CompilerParams(
            dimension_semantics=("arbitrary",), vmem_limit_bytes=VMEM_LIMIT),
        name="attention",
    )(qi_tab, ki_tab, page_table, qt, kb, vt, wvec, lam_params, subln_g.reshape(V_DIM, 1),
      qrows, k_new.reshape(nb, t, 512), v_new.reshape(nb, t, 512), b_last, b_new,
      subln_g.reshape(1, V_DIM), kt, v2)
    return a_p, a_s.reshape(nb * t, 512)


def _mix_tail(x, a16, pooled, u_now, wpool_ref, ps_ref, wout_ref, ging_ref, binb_ref,
              g1_ref, b1_ref, wr_ref, br_ref,
              h1_ref, h1t_ref, e_ref, gt_ref, rk_ref, cnt_ref, carry_s):
    i = pl.program_id(0)
    tm = x.shape[0]
    hp = _ln(x, ging_ref[...], binb_ref[...])
    pooled = pooled - u_now
    zs = [jnp.dot(pooled[:, g * 128:(g + 1) * 128].astype(BF16), wpool_ref[g],
                  preferred_element_type=F32) for g in range(4)]
    z = jnp.concatenate(zs, axis=1) * ps_ref[...]
    mix = (jnp.dot(a16, wout_ref[0:512, :], preferred_element_type=F32)
           + jnp.dot(z.astype(BF16), wout_ref[512:1024, :], preferred_element_type=F32))
    h1 = _ln(DN_ALPHA * hp + mix, g1_ref[...], b1_ref[...])
    h1_ref[...] = h1
    _to_row_tiles(h1t_ref, h1)

    x_hi = h1.astype(BF16)
    x_lo = (h1 - x_hi.astype(F32)).astype(BF16)
    w = wr_ref[...]
    w_hi = w.astype(BF16)
    w_lo = (w - w_hi.astype(F32)).astype(BF16)
    logits = (jnp.dot(x_hi, w_hi, preferred_element_type=F32)
              + jnp.dot(x_lo, w_hi, preferred_element_type=F32)
              + jnp.dot(x_hi, w_lo, preferred_element_type=F32)) + br_ref[...]
    lane = lax.broadcasted_iota(I32, (tm, LANES), 1).astype(F32)
    cur = jnp.where(lane < N_EXPERTS, logits, NEG_BIG)
    vals, idxs = [], []
    for _ in range(TOP_K):
        mx = jnp.max(cur, axis=1, keepdims=True)
        idx = jnp.min(jnp.where(cur == mx, lane, float(LANES)), axis=1, keepdims=True)
        vals.append(mx)
        idxs.append(idx)
        cur = jnp.where(lane == idx, NEG_BIG, cur)
    ex = [jnp.exp(v - vals[0]) for v in vals]
    den = ex[0] + ex[1] + ex[2] + ex[3]
    gates = [e / den for e in ex]

    @pl.when(i == 0)
    def _zero():
        carry_s[...] = jnp.zeros_like(carry_s)

    sel = [lane == idx for idx in idxs]
    member = jnp.zeros((tm, LANES), F32)
    for s_ in sel:
        member = jnp.where(s_, 1.0, member)
    r_id = lax.broadcasted_iota(I32, (tm, tm), 0)
    c_id = lax.broadcasted_iota(I32, (tm, tm), 1)
    lower = jnp.where(c_id < r_id, 1.0, 0.0).astype(BF16)
    pref = jnp.dot(lower, member.astype(BF16), preferred_element_type=F32) + carry_s[...]
    ranks = [jnp.sum(jnp.where(s_, pref, 0.0), axis=1, keepdims=True) for s_ in sel]
    carry_s[...] = carry_s[...] + jnp.sum(member, axis=0, keepdims=True)
    cnt_ref[...] = carry_s[...].astype(I32)

    e_out = jnp.zeros((tm, LANES), F32)
    g_out = jnp.zeros((tm, LANES), F32)
    r_out = jnp.zeros((tm, LANES), F32)
    for k in range(TOP_K):
        at_k = lane == float(k)
        e_out = jnp.where(at_k, idxs[k], e_out)
        g_out = jnp.where(at_k, gates[k], g_out)
        r_out = jnp.where(at_k, ranks[k], r_out)
    e_ref[...] = e_out.astype(I32)
    gt_ref[...] = g_out
    rk_ref[...] = r_out.astype(I32)


def _mix_kernel(xp_ref, ap_ref, u_ref, up_ref, xs_ref, as_ref, ue_ref, *refs, n_prompt_blocks):
    ext_s = refs[-1]
    refs = refs[:-1]
    i = pl.program_id(0)

    @pl.when(i < n_prompt_blocks)
    def _prompt():
        tm = xp_ref.shape[0]
        ext_s[0:16, :] = jnp.where(i == 0, 0.0, up_ref[...])
        u = u_ref[...]
        ext_s[16:16 + tm, :] = u
        pos = i * tm + lax.broadcasted_iota(I32, (tm, 1), 0)
        means = []
        for g, w in enumerate(POOL_WINDOWS):
            sl = slice(g * POOL_GROUP, (g + 1) * POOL_GROUP)
            s = u[:, sl]
            for j in range(1, w):
                s = s + ext_s[16 - j:16 - j + tm, sl]
            cnt = jnp.minimum(pos + 1, w).astype(F32)
            means.append(s / cnt)
        pooled = jnp.concatenate(means, axis=1)
        _mix_tail(xp_ref[...], ap_ref[...], pooled, u, *refs)

    @pl.when(i >= n_prompt_blocks)
    def _sample():
        bs = ue_ref.shape[0]
        t = ue_ref.shape[1] - 16
        u = ue_ref[:, 16:16 + t, :]
        means = []
        for g, w in enumerate(POOL_WINDOWS):
            sl = slice(g * POOL_GROUP, (g + 1) * POOL_GROUP)
            s = u[:, :, sl]
            for j in range(1, w):
                s = s + ue_ref[:, 16 - j:16 - j + t, sl]
            means.append(s / float(w))
        pooled = jnp.concatenate(means, axis=2).reshape(bs * t, POOL_WIDTH)
        _mix_tail(xs_ref[...], as_ref[...].astype(BF16), pooled, u.reshape(bs * t, POOL_WIDTH),
                  *refs)


def _mix(xp, a_p, u_p, xs, a_s, u_ext, params, *, tm):
    t_p, t_s = xp.shape[0], xs.shape[0]
    t = t_p + t_s
    npb = t_p // tm
    seq_t = u_ext.shape[1] - 16
    bs = tm // seq_t
    row = lambda i: (i, 0)
    const = lambda i: (0, 0)
    const3 = lambda i: (0, 0, 0)
    prow = lambda i: (jnp.minimum(i, npb - 1), 0)
    srow = lambda i: (jnp.maximum(i - npb, 0), 0)
    (wpool16, pscale, wout16, ln_in_g, ln_in_b, ln1_g, ln1_b, wr_pad, br_pad) = params
    param_specs = [pl.BlockSpec((4, POOL_GROUP, POOL_GROUP), const3),
                   pl.BlockSpec((1, POOL_WIDTH), const),
                   pl.BlockSpec((D_MODEL, D_MODEL), const),
                   pl.BlockSpec((1, D_MODEL), const), pl.BlockSpec((1, D_MODEL), const),
                   pl.BlockSpec((1, D_MODEL), const), pl.BlockSpec((1, D_MODEL), const),
                   pl.BlockSpec((D_MODEL, LANES), const), pl.BlockSpec((1, LANES), const)]
    data = [xp, a_p, u_p, u_p, xs, a_s, u_ext]
    data_specs = [pl.BlockSpec((tm, D_MODEL), prow), pl.BlockSpec((tm, 512), prow),
                  pl.BlockSpec((tm, 512), prow),
                  pl.BlockSpec((16, 512),
                               lambda i: (jnp.maximum(jnp.minimum(i, npb - 1) * (tm // 16) - 1, 0), 0)),
                  pl.BlockSpec((tm, D_MODEL), srow), pl.BlockSpec((tm, 512), srow),
                  pl.BlockSpec((bs, 16 + seq_t, 512), lambda i: (jnp.maximum(i - npb, 0), 0, 0))]
    scratch = [pltpu.VMEM((1, LANES), F32), pltpu.VMEM((16 + tm, 512), F32)]
    out_shape = [jax.ShapeDtypeStruct((t, D_MODEL), F32),
                 jax.ShapeDtypeStruct((t * ROW_TILES, LANES), F32),
                 jax.ShapeDtypeStruct((t, LANES), I32),
                 jax.ShapeDtypeStruct((t, LANES), F32),
                 jax.ShapeDtypeStruct((t, LANES), I32),
                 jax.ShapeDtypeStruct((1, LANES), I32)]
    out_specs = [pl.BlockSpec((tm, D_MODEL), row), pl.BlockSpec((tm * ROW_TILES, LANES), row),
                 pl.BlockSpec((tm, LANES), row),
                 pl.BlockSpec((tm, LANES), row), pl.BlockSpec((tm, LANES), row),
                 pl.BlockSpec((1, LANES), const)]
    return pl.pallas_call(
        functools.partial(_mix_kernel, n_prompt_blocks=npb),
        grid=(t // tm,), in_specs=data_specs + param_specs, out_specs=out_specs,
        out_shape=out_shape, scratch_shapes=scratch,
        compiler_params=pltpu.CompilerParams(
            dimension_semantics=("arbitrary",), vmem_limit_bytes=VMEM_LIMIT),
        name="mix",
    )(*data, wpool16, pscale, wout16, ln_in_g, ln_in_b, ln1_g, ln1_b, wr_pad, br_pad)


def _tile_copy(src, src_row, dst, dst_row, sem):
    return pltpu.make_async_copy(
        src.at[pl.ds(pl.multiple_of(src_row * ROW_TILES, ROW_TILES), ROW_TILES)],
        dst.at[pl.ds(pl.multiple_of(dst_row * ROW_TILES, ROW_TILES), ROW_TILES)], sem)


def _experts_kernel(be_ref, nu_ref, tok0_ref, tokn_ref, dst_ref, x_hbm, wgu_ref, bgu_ref,
                    wd_ref, bd_ref, yc_hbm, wgu16_s, wd16_s, xbuf, ybuf, gsem, ssem, *, tm):
    i = pl.program_id(0)
    nu = nu_ref[0]
    slot = i % 2
    e = be_ref[i]
    prev = be_ref[jnp.maximum(i - 1, 0)]
    half = tm // 2

    def gather_start(tok_ref, sl):
        for r in range(tm):
            _tile_copy(x_hbm, tok_ref[0, 0, r], xbuf.at[sl], r, gsem.at[sl]).start(priority=r % 2)

    def gather_wait(sl):
        for r in range(tm):
            _tile_copy(x_hbm, 0, xbuf.at[sl], r, gsem.at[sl]).wait()

    def scatter(sl, wait):
        for r in range(tm):
            cp = _tile_copy(ybuf.at[sl], r, yc_hbm, dst_ref[0, 0, r], ssem)
            cp.wait() if wait else cp.start(priority=r % 2)

    @pl.when(i == 0)
    def _prime():
        gather_start(tok0_ref, 0)
        ybuf[1] = jnp.zeros(ybuf.shape[1:], F32)

    @pl.when((i == 0) | (e != prev))
    def _cast():
        wgu16_s[...] = wgu_ref[0].astype(BF16)
        wd16_s[...] = wd_ref[0].astype(BF16)

    @pl.when(i < nu)
    def _run():
        gather_wait(slot)
        gather_start(tokn_ref, 1 - slot)
        scatter(1 - slot, wait=False)
        x = _from_row_tiles(xbuf.at[slot], tm).astype(BF16)
        gus = [jnp.dot(x[hh * half:(hh + 1) * half], wgu16_s[...],
                       preferred_element_type=F32) + bgu_ref[0] for hh in range(2)]
        for hh in range(2):
            gate = jnp.minimum(gus[hh][:, 0:D_FF], SWIGLU_LIMIT)
            up = jnp.clip(gus[hh][:, D_FF:2 * D_FF], -SWIGLU_LIMIT, SWIGLU_LIMIT)
            hdn = (up + 1.0) * (gate * jax.nn.sigmoid(SWIGLU_ALPHA * gate))
            y = jnp.dot(hdn.astype(BF16), wd16_s[...], preferred_element_type=F32) + bd_ref[0]
            for c in range(ROW_TILES):
                ybuf[slot, pl.ds(hh * half * ROW_TILES + c, half, stride=ROW_TILES), :] = (
                    y[:, c * LANES:(c + 1) * LANES])
        scatter(1 - slot, wait=True)

    @pl.when(i == nu)
    def _flush():
        gather_wait(slot)
        scatter(1 - slot, wait=False)
        scatter(1 - slot, wait=True)


def _moe_experts(h1t, row_tok, row_dst_prev, block_e, n_used, w_gu, b_gu, w_down, b_down,
                 *, tm, n_out_rows):
    rows = row_tok.shape[0]
    nblk = rows // tm
    tok3 = row_tok.reshape(nblk, 1, tm)
    smem_blk = lambda f: pl.BlockSpec((1, 1, tm), f, memory_space=pltpu.SMEM)
    grid_spec = pltpu.PrefetchScalarGridSpec(
        num_scalar_prefetch=2, grid=(nblk,),
        in_specs=[smem_blk(lambda i, be, nu: (0, 0, 0)),
                  smem_blk(lambda i, be, nu: (jnp.minimum(i + 1, nblk - 1), 0, 0)),
                  smem_blk(lambda i, be, nu: (i, 0, 0)),
                  pl.BlockSpec(memory_space=pl.ANY),
                  pl.BlockSpec((1, D_MODEL, 2 * D_FF), lambda i, be, nu: (be[i], 0, 0)),
                  pl.BlockSpec((1, 1, 2 * D_FF), lambda i, be, nu: (be[i], 0, 0)),
                  pl.BlockSpec((1, D_FF, D_MODEL), lambda i, be, nu: (be[i], 0, 0)),
                  pl.BlockSpec((1, 1, D_MODEL), lambda i, be, nu: (be[i], 0, 0))],
        out_specs=pl.BlockSpec(memory_space=pl.ANY),
        scratch_shapes=[pltpu.VMEM((D_MODEL, 2 * D_FF), BF16),
                        pltpu.VMEM((D_FF, D_MODEL), BF16),
                        pltpu.VMEM((2, tm * ROW_TILES, LANES), F32),
                        pltpu.VMEM((2, tm * ROW_TILES, LANES), F32),
                        pltpu.SemaphoreType.DMA((2,)),
                        pltpu.SemaphoreType.DMA])
    return pl.pallas_call(
        functools.partial(_experts_kernel, tm=tm), grid_spec=grid_spec,
        out_shape=jax.ShapeDtypeStruct((n_out_rows * ROW_TILES, LANES), F32),
        compiler_params=pltpu.CompilerParams(
            dimension_semantics=("arbitrary",), vmem_limit_bytes=VMEM_LIMIT),
        name="moe_experts",
    )(block_e, n_used, tok3, tok3, row_dst_prev.reshape(nblk, 1, tm), h1t,
      w_gu, b_gu.reshape(N_EXPERTS, 1, 2 * D_FF), w_down, b_down.reshape(N_EXPERTS, 1, D_MODEL))


def _finish_kernel(y0_ref, y1_ref, y2_ref, y3_ref, h1_ref, gt_ref, g2_ref, b2_ref,
                   oa_ref, ob_ref, *, tm, nblk_a):
    i = pl.program_id(0)
    gt = gt_ref[...]
    f = _from_row_tiles(y0_ref, tm) * gt[:, 0:1]
    for k, y_ref in enumerate((y1_ref, y2_ref, y3_ref), start=1):
        f = f + _from_row_tiles(y_ref, tm) * gt[:, k:k + 1]
    y = _ln(DN_ALPHA * h1_ref[...] + f, g2_ref[...], b2_ref[...])

    @pl.when(i < nblk_a)
    def _first():
        oa_ref[...] = y

    @pl.when(i >= nblk_a)
    def _second():
        ob_ref[...] = y


def _moe_finish(yc, h1, gates, ln2_g, ln2_b, *, tm, t_first):
    t = h1.shape[0]
    nblk = t // tm
    nblk_a = t_first // tm
    row = lambda i: (i, 0)
    const = lambda i: (0, 0)
    slot_spec = lambda k: pl.BlockSpec((tm * ROW_TILES, LANES), lambda i: (k * nblk + i, 0))
    return pl.pallas_call(
        functools.partial(_finish_kernel, tm=tm, nblk_a=nblk_a), grid=(nblk,),
        in_specs=[slot_spec(0), slot_spec(1), slot_spec(2), slot_spec(3),
                  pl.BlockSpec((tm, D_MODEL), row),
                  pl.BlockSpec((tm, LANES), row),
                  pl.BlockSpec((1, D_MODEL), const), pl.BlockSpec((1, D_MODEL), const)],
        out_specs=[pl.BlockSpec((tm, D_MODEL), lambda i: (jnp.minimum(i, nblk_a - 1), 0)),
                   pl.BlockSpec((tm, D_MODEL), lambda i: (jnp.maximum(i - nblk_a, 0), 0))],
        out_shape=[jax.ShapeDtypeStruct((t_first, D_MODEL), F32),
                   jax.ShapeDtypeStruct((t - t_first, D_MODEL), F32)],
        compiler_params=pltpu.CompilerParams(
            dimension_semantics=("arbitrary",), vmem_limit_bytes=VMEM_LIMIT),
        name="moe_finish",
    )(yc, yc, yc, yc, h1, gates, ln2_g, ln2_b)


def _moe(h1, h1t, top_e, gates, rank, counts, w_gu, b_gu, w_down, b_down, ln2_g, ln2_b,
         *, tm_e, tm_c, t_first):
    t = h1.shape[0]
    counts = counts[0, :N_EXPERTS]
    padded = (counts + tm_e - 1) // tm_e * tm_e
    pad_end = jnp.cumsum(padded)
    pad_start = pad_end - padded
    e4 = top_e[:, :TOP_K]
    onehot = e4[:, :, None] == jnp.arange(N_EXPERTS, dtype=I32)[None, None, :]
    dest = rank[:, :TOP_K] + jnp.sum(jnp.where(onehot, pad_start[None, None, :], 0), axis=-1)
    nblk = (t * TOP_K) // tm_e + N_EXPERTS
    rows = nblk * tm_e
    starts = jnp.arange(nblk, dtype=I32) * tm_e
    block_e = jnp.minimum(jnp.sum((pad_end[None, :] <= starts[:, None]).astype(I32), axis=1),
                          N_EXPERTS - 1).astype(I32)
    nu_e = (pad_end[-1] // tm_e).astype(I32).reshape(1)
    flat = jnp.arange(t * TOP_K, dtype=I32)
    code = (flat % TOP_K) * t + flat // TOP_K
    inv = jnp.full((rows,), -1, I32).at[dest.reshape(-1)].set(code, unique_indices=True)
    spare = TOP_K * t + jnp.arange(rows, dtype=I32) % tm_e
    row_tok = jnp.where(inv < 0, 0, inv % t)
    row_dst = jnp.where(inv < 0, spare, inv)
    row_dst_prev = jnp.concatenate([spare[:tm_e], row_dst[:rows - tm_e]])
    yc = _moe_experts(h1t, row_tok, row_dst_prev, block_e, nu_e, w_gu, b_gu, w_down, b_down,
                      tm=tm_e, n_out_rows=TOP_K * t + tm_e)
    return _moe_finish(yc, h1, gates, ln2_g, ln2_b, tm=tm_c, t_first=t_first)


def kernel(x_prompt, x_sample, cache_k, cache_v, state_pool, page_table, ln_in_g, ln_in_b, w_in, lambda_q1, lambda_k1, lambda_q2, lambda_k2, subln_g, rel_bias_table, w_pool, pool_scale, w_out, ln1_g, ln1_b, w_router, b_router, w_gate_up, b_gate_up, w_down, b_down, ln2_g, ln2_b):
    l = 0
    bsz, seq, _ = x_prompt.shape
    nb, dec_t, _ = x_sample.shape
    assert bsz == 1
    r2 = lambda a: a.reshape(1, -1)
    w_in16 = w_in[l].astype(BF16)
    lam_params = jnp.stack([lambda_q1[l], lambda_k1[l], lambda_q2[l], lambda_k2[l]]).astype(F32)
    wr_pad = jnp.pad(w_router[l], ((0, 0), (0, LANES - N_EXPERTS)))
    br_pad = jnp.pad(b_router[l], (0, LANES - N_EXPERTS)).reshape(1, LANES)
    mix_params = (w_pool[l].astype(BF16), r2(pool_scale[l]), w_out[l].astype(BF16),
                  r2(ln_in_g), r2(ln_in_b), r2(ln1_g[l]), r2(ln1_b[l]), wr_pad, br_pad)

    xp = x_prompt.reshape(seq, D_MODEL)
    xs = x_sample.reshape(nb * dec_t, D_MODEL)
    k_p, v_p, u_p, qt_p, kb_p, vt_p = _inproj(xp, r2(ln_in_g), r2(ln_in_b), w_in16,
                                              tm=512, prompt=True)
    q_s, k_s, v_s, u_s = _inproj(xs, r2(ln_in_g), r2(ln_in_b), w_in16, tm=512, prompt=False)
    a_p, a_s = _attention(qt_p, kb_p, vt_p, q_s, k_s, v_s, cache_k[l], cache_v[l], page_table,
                          rel_bias_table, lam_params, subln_g[l], tq=512, pg=16)

    u_ext = jnp.concatenate([jnp.zeros((nb, 1, POOL_WIDTH), F32), state_pool[l],
                             u_s.reshape(nb, dec_t, POOL_WIDTH)], axis=1)
    h1, h1t, top_e, gates, rank, cnt = _mix(xp, a_p, u_p, xs, a_s, u_ext, mix_params, tm=512)
    y_p, y_s = _moe(h1, h1t, top_e, gates, rank, cnt, w_gate_up[l], b_gate_up[l], w_down[l],
                    b_down[l], r2(ln2_g[l]), r2(ln2_b[l]), tm_e=256, tm_c=256, t_first=seq)

    k_prompt = k_p.reshape(1, bsz, seq, N_HEADS, 2, QK_DIM)
    v_prompt = v_p.reshape(1, bsz, seq, N_HEADS, V_DIM)
    pool_prompt = u_p[seq - POOL_STATE:].reshape(1, bsz, POOL_STATE, POOL_WIDTH)
    k_sample = k_s.reshape(1, nb, dec_t, N_HEADS, 2, QK_DIM)
    v_sample = v_s.reshape(1, nb, dec_t, N_HEADS, V_DIM)
    pool_sample = u_ext[:, 16 + dec_t - POOL_STATE:].reshape(1, nb, POOL_STATE, POOL_WIDTH)
    return (y_p.reshape(bsz, seq, D_MODEL), y_s.reshape(nb, dec_t, D_MODEL),
            k_prompt, v_prompt, pool_prompt, k_sample, v_sample, pool_sample)
```

```python
import functools
import math

import jax
import jax.numpy as jnp
from jax import lax
from jax.experimental import pallas as pl
from jax.experimental.pallas import tpu as pltpu

F32 = jnp.float32
BF16 = jnp.bfloat16
I32 = jnp.int32

D_MODEL = 1024
PAGE_SIZE = 128
N_HEADS = 4
V_DIM = 128
QK_DIM = 64
ATTN_WIDTH = N_HEADS * V_DIM
POOL_WIDTH = 512
QK_SCALE = QK_DIM ** -0.5
IN_WIDTH = 4 * ATTN_WIDTH
POOL_WINDOWS = (2, 4, 8, 16)
POOL_GROUP = 128
POOL_STATE = 15
NUM_BUCKETS = 32
MAX_DISTANCE = 128
N_EXPERTS = 32
TOP_K = 4
D_FF = D_MODEL
SWIGLU_LIMIT = 7.0
SWIGLU_ALPHA = 1.702
DEPTH = 1
DN_ALPHA = (2.0 * DEPTH) ** 0.25
LN_EPS = 1e-5
LAMBDA_INIT = 0.8 - 0.6 * math.exp(-0.3 * 0)

LANES = 128
SUBLANES = 8
ROW_TILES = D_MODEL // LANES
NEG_BIG = -1e30
LOG2E = math.log2(math.e)
VMEM_LIMIT = 56 * 1024 * 1024
BF16_SUBLANES = 16
VT_ROWS = V_DIM + BF16_SUBLANES


def _ln(x, g, b):
    mu = jnp.mean(x, axis=-1, keepdims=True)
    xc = x - mu
    var = jnp.mean(xc * xc, axis=-1, keepdims=True)
    return xc * lax.rsqrt(var + LN_EPS) * g + b


def _lam(lp_ref):
    lp = lp_ref[...]
    s1 = jnp.sum(lp[0:1, :] * lp[1:2, :], axis=-1, keepdims=True)
    s2 = jnp.sum(lp[2:3, :] * lp[3:4, :], axis=-1, keepdims=True)
    return jnp.exp(s1) - jnp.exp(s2) + LAMBDA_INIT


def _to_row_tiles(ref, x):
    n = x.shape[0]
    for c in range(ROW_TILES):
        ref[pl.ds(c, n, stride=ROW_TILES), :] = x[:, c * LANES:(c + 1) * LANES]


def _from_row_tiles(ref, n, first=0):
    return jnp.concatenate(
        [ref[pl.ds(first * ROW_TILES + c, n, stride=ROW_TILES), :] for c in range(ROW_TILES)],
        axis=1)


def _inproj_prompt_kernel(x_ref, g_ref, b_ref, w_ref,
                          k_ref, v_ref, u_ref, qt_ref, kb_ref, vt_ref):
    h = _ln(x_ref[...], g_ref[...], b_ref[...]).astype(BF16)
    p = jnp.dot(h, w_ref[...], preferred_element_type=F32)
    q = p[:, 0:512] * (QK_SCALE * LOG2E)
    k = p[:, 512:1024]
    v = p[:, 1024:1536]
    k_ref[...] = k
    v_ref[...] = v
    u_ref[...] = p[:, 1536:2048]
    kb_ref[...] = k.astype(BF16)
    qt_ref[...] = q.T.astype(BF16)
    vt = v.T.astype(BF16)
    ones = jnp.ones((BF16_SUBLANES, vt.shape[1]), BF16)
    for hd in range(N_HEADS):
        vt_ref[hd * VT_ROWS:hd * VT_ROWS + V_DIM, :] = vt[hd * V_DIM:(hd + 1) * V_DIM, :]
        vt_ref[hd * VT_ROWS + V_DIM:(hd + 1) * VT_ROWS, :] = ones


def _inproj_sample_kernel(x_ref, g_ref, b_ref, w_ref, q_ref, k_ref, v_ref, u_ref):
    h = _ln(x_ref[...], g_ref[...], b_ref[...]).astype(BF16)
    p = jnp.dot(h, w_ref[...], preferred_element_type=F32)
    q_ref[...] = p[:, 0:512] * (QK_SCALE * LOG2E)
    k_ref[...] = p[:, 512:1024]
    v_ref[...] = p[:, 1024:1536]
    u_ref[...] = p[:, 1536:2048]


def _inproj(x, g, b, w16, *, tm, prompt):
    t = x.shape[0]
    row = lambda i: (i, 0)
    col = lambda i: (0, i)
    const = lambda i: (0, 0)
    in_specs = [pl.BlockSpec((tm, D_MODEL), row),
                pl.BlockSpec((1, D_MODEL), const),
                pl.BlockSpec((1, D_MODEL), const),
                pl.BlockSpec((D_MODEL, IN_WIDTH), const)]
    f32_out = jax.ShapeDtypeStruct((t, 512), F32)
    if prompt:
        out_shape = [f32_out, f32_out, f32_out,
                     jax.ShapeDtypeStruct((512, t), BF16),
                     jax.ShapeDtypeStruct((t, 512), BF16),
                     jax.ShapeDtypeStruct((N_HEADS * VT_ROWS, t), BF16)]
        out_specs = [pl.BlockSpec((tm, 512), row)] * 3 + [
            pl.BlockSpec((512, tm), col), pl.BlockSpec((tm, 512), row),
            pl.BlockSpec((N_HEADS * VT_ROWS, tm), col)]
        body = _inproj_prompt_kernel
    else:
        out_shape = [f32_out] * 4
        out_specs = [pl.BlockSpec((tm, 512), row)] * 4
        body = _inproj_sample_kernel
    return pl.pallas_call(
        body, grid=(t // tm,), in_specs=in_specs, out_specs=out_specs,
        out_shape=out_shape,
        compiler_params=pltpu.CompilerParams(
            dimension_semantics=("arbitrary",), vmem_limit_bytes=VMEM_LIMIT),
        name="inproj_prompt" if prompt else "inproj_sample",
    )(x, g, b, w16)


def _bucket(dist):
    max_exact = NUM_BUCKETS // 2
    df = jnp.maximum(dist, 1).astype(F32)
    large = max_exact + (jnp.log(df / max_exact) / math.log(MAX_DISTANCE / max_exact)
                         * (NUM_BUCKETS - max_exact)).astype(I32)
    large = jnp.minimum(large, NUM_BUCKETS - 1)
    return jnp.where(dist < max_exact, dist, large)


def _bias_rel(table, dist):
    far = table[NUM_BUCKETS - 1].astype(F32)
    b = (jnp.take(table.astype(F32), _bucket(dist), axis=0) - far) * LOG2E
    return jnp.moveaxis(b, -1, 0)


def _prompt_attn_step(i, qi_tab, ki_tab, qt_ref, k_ref, vt_ref, wv_ref, lp_ref, g_ref,
                      o_ref, qm_s, acc_s, m_s, l_s, bias_s, tq):
    qi = qi_tab[i]
    ki = ki_tab[i]

    @pl.when(i == 0)
    def _bias():
        for t in range(2 * N_HEADS):
            x = jnp.broadcast_to(wv_ref[t:t + 1, :], (tq, 2 * tq))
            r = pltpu.roll(x, 0, 1, stride=1, stride_axis=0)
            bias_s[t] = r[:, tq:2 * tq]

    @pl.when(ki == 0)
    def _init():
        rowid = lax.broadcasted_iota(I32, (V_DIM, tq), 0)
        for h in range(N_HEADS):
            qh = qt_ref[h * 128:(h + 1) * 128, :].astype(F32)
            qm_s[2 * h] = jnp.where(rowid < QK_DIM, qh, 0.0).astype(BF16)
            qm_s[2 * h + 1] = jnp.where(rowid >= QK_DIM, qh, 0.0).astype(BF16)
        acc_s[...] = jnp.zeros_like(acc_s)
        l_s[...] = jnp.zeros_like(l_s)
        m_s[...] = jnp.full_like(m_s, NEG_BIG)

    def step(bias_base):
        def scores(j):
            h = j // 2
            s = jnp.dot(k_ref[:, h * 128:(h + 1) * 128], qm_s[j],
                        preferred_element_type=F32)
            if bias_base is not None:
                s = s + bias_s[bias_base + h]
            return s

        pending = [scores(0), scores(1)]
        for j in range(2 * N_HEADS):
            h = j // 2
            s = pending.pop(0)
            m_old = m_s[j:j + 1, :]
            m_new = jnp.maximum(m_old, jnp.max(s, axis=0, keepdims=True))
            alpha = jnp.exp2(m_old - m_new)
            p16 = jnp.exp2((s - m_new).astype(BF16))
            m_s[j:j + 1, :] = m_new
            if j + 2 < 2 * N_HEADS:
                pending.append(scores(j + 2))
            pv = jnp.dot(vt_ref[h * VT_ROWS:(h + 1) * VT_ROWS, :], p16,
                         preferred_element_type=F32)
            l_s[j:j + 1, :] = alpha * l_s[j:j + 1, :] + pv[V_DIM:V_DIM + 1, :]
            acc_s[j] = alpha * acc_s[j] + pv[0:V_DIM, :]

    @pl.when(ki < qi - 1)
    def _far():
        step(None)

    @pl.when(ki == qi - 1)
    def _sub():
        step(0)

    @pl.when(ki == qi)
    def _diag():
        step(N_HEADS)
        lam = _lam(lp_ref)
        for h in range(N_HEADS):
            o1 = acc_s[2 * h] * (1.0 / l_s[2 * h:2 * h + 1, :])
            o2 = acc_s[2 * h + 1] * (1.0 / l_s[2 * h + 1:2 * h + 2, :])
            a = o1 - lam * o2
            ms = jnp.mean(a * a, axis=0, keepdims=True)
            a = a * lax.rsqrt(ms + LN_EPS) * g_ref[...] * (1.0 - LAMBDA_INIT)
            o_ref[:, h * 128:(h + 1) * 128] = a.T.astype(BF16)


def _sample_page_copies(st, sl, pt_ref, kt_hbm, v_hbm, kbuf, vbuf, ksem, vsem, pg, n_chunks):
    bb = st // n_chunks
    cc = st % n_chunks
    cps = []
    for j in range(pg):
        page = pt_ref[bb, cc * pg + j]
        cps.append(pltpu.make_async_copy(kt_hbm.at[page], kbuf.at[sl, j], ksem.at[sl]))
        cps.append(pltpu.make_async_copy(v_hbm.at[page], vbuf.at[sl, j], vsem.at[sl]))
    return cps


def _sample_attn_step(step, q_ref, kn_ref, vn_ref, bl_ref, bn_ref, lp_ref, g_ref, o_ref,
                      kbuf, vbuf, acc_s, m_s, l_s, pg, n_chunks):
    c = step % n_chunks
    slot = step % 2
    last = c == n_chunks - 1

    @pl.when(c == 0)
    def _init():
        acc_s[...] = jnp.zeros_like(acc_s)
        l_s[...] = jnp.zeros_like(l_s)
        m_s[...] = jnp.full_like(m_s, NEG_BIG)

    qr = q_ref[0]
    kcat = jnp.concatenate([kbuf[slot, j].astype(BF16) for j in range(pg)], axis=1)
    s = jnp.dot(qr, kcat, preferred_element_type=F32)
    s = jnp.concatenate([s[:, :(pg - 1) * PAGE_SIZE],
                         s[:, (pg - 1) * PAGE_SIZE:] + jnp.where(last, bl_ref[...], 0.0)], axis=1)
    m_old = m_s[...]
    m_new = jnp.maximum(m_old, jnp.max(s, axis=1, keepdims=True))
    alpha = jnp.exp2(m_old - m_new)
    p = jnp.exp2(s - m_new)
    l_s[...] = alpha * l_s[...] + jnp.sum(p, axis=1, keepdims=True)
    vcat = jnp.concatenate(
        [jnp.concatenate([vbuf[slot, j, pl.ds(h, PAGE_SIZE, stride=N_HEADS), :]
                          for h in range(N_HEADS)], axis=1).astype(BF16)
         for j in range(pg)], axis=0)
    acc_s[...] = alpha * acc_s[...] + jnp.dot(p.astype(BF16), vcat, preferred_element_type=F32)
    m_s[...] = m_new

    @pl.when(last)
    def _fin():
        nt = (((1,), (1,)), ((), ()))
        kn = kn_ref[0].astype(BF16).astype(F32)
        vn = vn_ref[0].astype(BF16).astype(F32)
        s_new = lax.dot_general(qr.astype(F32), kn, nt,
                                preferred_element_type=F32) + bn_ref[...]
        m0 = m_s[...]
        m1 = jnp.maximum(m0, jnp.max(s_new, axis=1, keepdims=True))
        a1 = jnp.exp2(m0 - m1)
        p_new = jnp.exp2(s_new - m1)
        l1 = a1 * l_s[...] + jnp.sum(p_new, axis=1, keepdims=True)
        p_new = p_new.astype(BF16).astype(F32)
        acc = a1 * acc_s[...] + jnp.dot(p_new, vn, preferred_element_type=F32)
        o = acc * (1.0 / l1)
        lam = _lam(lp_ref)
        for h in range(N_HEADS):
            blk = o[h * 16:(h + 1) * 16, h * 128:(h + 1) * 128]
            a = blk[0:8, :] - lam * blk[8:16, :]
            ms = jnp.mean(a * a, axis=-1, keepdims=True)
            a = a * lax.rsqrt(ms + LN_EPS) * g_ref[...] * (1.0 - LAMBDA_INIT)
            o_ref[0, :, h * 128:(h + 1) * 128] = a


def _attn_kernel(qi_tab, ki_tab, pt_ref,
                 qt_ref, k_ref, vt_ref, wv_ref, lp_ref, gcol_ref,
                 qs_ref, kn_ref, vn_ref, bl_ref, bn_ref, grow_ref, kt_hbm, v_hbm,
                 op_ref, os_ref,
                 qm_s, acc_s, m_s, l_s, bias_s, kbuf, vbuf, ksem, vsem, sacc_s, sm_s, sl_s,
                 *, tq, pg, n_chunks, n_sample_steps):
    i = pl.program_id(0)
    slot = i % 2
    copies = functools.partial(_sample_page_copies, pt_ref=pt_ref, kt_hbm=kt_hbm, v_hbm=v_hbm,
                               kbuf=kbuf, vbuf=vbuf, ksem=ksem, vsem=vsem, pg=pg,
                               n_chunks=n_chunks)

    @pl.when(i == 0)
    def _prime():
        for cp in copies(0, 0):
            cp.start()

    @pl.when(i + 1 < n_sample_steps)
    def _prefetch():
        for cp in copies(i + 1, 1 - slot):
            cp.start()

    _prompt_attn_step(i, qi_tab, ki_tab, qt_ref, k_ref, vt_ref, wv_ref, lp_ref, gcol_ref,
                      op_ref, qm_s, acc_s, m_s, l_s, bias_s, tq)

    @pl.when(i < n_sample_steps)
    def _sample():
        for cp in copies(i, slot):
            cp.wait()
        _sample_attn_step(i, qs_ref, kn_ref, vn_ref, bl_ref, bn_ref, lp_ref, grow_ref, os_ref,
                          kbuf, vbuf, sacc_s, sm_s, sl_s, pg, n_chunks)


def _attention(qt, kb, vt, q_s, k_new, v_new, cache_k, cache_v, page_table, table, lam_params,
               subln_g, *, tq, pg):
    t_p = kb.shape[0]
    nq = t_p // tq
    pairs = [(a, b) for a in range(nq) for b in range(a + 1)]
    qi_tab = jnp.asarray([p[0] for p in pairs], I32)
    ki_tab = jnp.asarray([p[1] for p in pairs], I32)
    c = jnp.arange(2 * tq)
    w_sub = _bias_rel(table, c)
    w_diag = jnp.where(c >= tq, _bias_rel(table, jnp.maximum(c - tq, 0)), NEG_BIG)
    wvec = jnp.concatenate([w_sub, w_diag], axis=0)

    nb, n_pages = page_table.shape
    t = q_s.shape[0] // nb
    n_chunks = n_pages // pg
    ns = nb * n_chunks
    assert ns <= len(pairs), "sample chunks must fit in the prompt's grid steps"
    past = n_pages * PAGE_SIZE
    n_pool = cache_k.shape[0]
    kt = jnp.transpose(cache_k, (0, 2, 3, 4, 1)).reshape(n_pool, 512, PAGE_SIZE)
    v2 = cache_v.reshape(n_pool, PAGE_SIZE * N_HEADS, V_DIM)
    q5 = q_s.reshape(nb, t, 8, QK_DIM)
    eye = jnp.eye(8, dtype=F32)
    qrows = (jnp.transpose(q5, (0, 2, 1, 3))[:, :, :, None, :] * eye[None, :, None, :, None])
    qrows = qrows.reshape(nb, 8 * t, 512).astype(BF16)
    tt = jnp.arange(t)
    jj = jnp.arange(PAGE_SIZE)
    d_last = (past + tt[:, None]) - (past - PAGE_SIZE + jj[None, :])
    b_last = _bias_rel(table, d_last)
    b_last = jnp.broadcast_to(b_last[:, None], (N_HEADS, 2, t, PAGE_SIZE)).reshape(8 * t, PAGE_SIZE)
    d_new = tt[:, None] - tt[None, :]
    b_new = jnp.where(d_new >= 0, _bias_rel(table, jnp.maximum(d_new, 0)), NEG_BIG)
    b_new = jnp.broadcast_to(b_new[:, None], (N_HEADS, 2, t, t)).reshape(8 * t, t)

    const = lambda i, qt_, kt_, pt_: (0, 0)
    seq3 = lambda i, qt_, kt_, pt_: (jnp.minimum(i, ns - 1) // n_chunks, 0, 0)
    in_specs = [
        pl.BlockSpec((512, tq), lambda i, qt_, kt_, pt_: (0, qt_[i])),
        pl.BlockSpec((tq, 512), lambda i, qt_, kt_, pt_: (kt_[i], 0)),
        pl.BlockSpec((N_HEADS * VT_ROWS, tq), lambda i, qt_, kt_, pt_: (0, kt_[i])),
        pl.BlockSpec((2 * N_HEADS, 2 * tq), const),
        pl.BlockSpec((4, QK_DIM), const),
        pl.BlockSpec((V_DIM, 1), const),
        pl.BlockSpec((1, 8 * t, 512), seq3),
        pl.BlockSpec((1, t, 512), seq3), pl.BlockSpec((1, t, 512), seq3),
        pl.BlockSpec((8 * t, PAGE_SIZE), const),
        pl.BlockSpec((8 * t, t), const),
        pl.BlockSpec((1, V_DIM), const),
        pl.BlockSpec(memory_space=pl.ANY),
        pl.BlockSpec(memory_space=pl.ANY)]
    grid_spec = pltpu.PrefetchScalarGridSpec(
        num_scalar_prefetch=3, grid=(len(pairs),), in_specs=in_specs,
        out_specs=[pl.BlockSpec((tq, 512), lambda i, qt_, kt_, pt_: (qt_[i], 0)),
                   pl.BlockSpec((1, t, 512), seq3)],
        scratch_shapes=[pltpu.VMEM((8, V_DIM, tq), BF16),
                        pltpu.VMEM((8, V_DIM, tq), F32),
                        pltpu.VMEM((8, tq), F32),
                        pltpu.VMEM((8, tq), F32),
                        pltpu.VMEM((2 * N_HEADS, tq, tq), F32),
                        pltpu.VMEM((2, pg, 512, PAGE_SIZE), F32),
                        pltpu.VMEM((2, pg, PAGE_SIZE * N_HEADS, V_DIM), F32),
                        pltpu.SemaphoreType.DMA((2,)),
                        pltpu.SemaphoreType.DMA((2,)),
                        pltpu.VMEM((8 * t, 512), F32),
                        pltpu.VMEM((8 * t, 1), F32),
                        pltpu.VMEM((8 * t, 1), F32)])
    a_p, a_s = pl.pallas_call(
        functools.partial(_attn_kernel, tq=tq, pg=pg, n_chunks=n_chunks, n_sample_steps=ns),
        grid_spec=grid_spec,
        out_shape=[jax.ShapeDtypeStruct((t_p, 512), BF16),
                   jax.ShapeDtypeStruct((nb, t, 512), F32)],
        compiler_params=pltpu.CompilerParams(
            dimension_semantics=("arbitrary",), vmem_limit_bytes=VMEM_LIMIT),
        name="attention",
    )(qi_tab, ki_tab, page_table, qt, kb, vt, wvec, lam_params, subln_g.reshape(V_DIM, 1),
      qrows, k_new.reshape(nb, t, 512), v_new.reshape(nb, t, 512), b_last, b_new,
      subln_g.reshape(1, V_DIM), kt, v2)
    return a_p, a_s.reshape(nb * t, 512)


def _mix_tail(x, a16, pooled, u_now, wpool_ref, ps_ref, wout_ref, ging_ref, binb_ref,
              g1_ref, b1_ref, wr_ref, br_ref,
              h1_ref, h1t_ref, e_ref, gt_ref, rk_ref, cnt_ref, carry_s):
    i = pl.program_id(0)
    tm = x.shape[0]
    hp = _ln(x, ging_ref[...], binb_ref[...])
    pooled = pooled - u_now
    zs = [jnp.dot(pooled[:, g * 128:(g + 1) * 128].astype(BF16), wpool_ref[g],
                  preferred_element_type=F32) for g in range(4)]
    z = jnp.concatenate(zs, axis=1) * ps_ref[...]
    mix = (jnp.dot(a16, wout_ref[0:512, :], preferred_element_type=F32)
           + jnp.dot(z.astype(BF16), wout_ref[512:1024, :], preferred_element_type=F32))
    h1 = _ln(DN_ALPHA * hp + mix, g1_ref[...], b1_ref[...])
    h1_ref[...] = h1
    _to_row_tiles(h1t_ref, h1)

    x_hi = h1.astype(BF16)
    x_lo = (h1 - x_hi.astype(F32)).astype(BF16)
    w = wr_ref[...]
    w_hi = w.astype(BF16)
    w_lo = (w - w_hi.astype(F32)).astype(BF16)
    logits = (jnp.dot(x_hi, w_hi, preferred_element_type=F32)
              + jnp.dot(x_lo, w_hi, preferred_element_type=F32)
              + jnp.dot(x_hi, w_lo, preferred_element_type=F32)) + br_ref[...]
    lane = lax.broadcasted_iota(I32, (tm, LANES), 1).astype(F32)
    cur = jnp.where(lane < N_EXPERTS, logits, NEG_BIG)
    vals, idxs = [], []
    for _ in range(TOP_K):
        mx = jnp.max(cur, axis=1, keepdims=True)
        idx = jnp.min(jnp.where(cur == mx, lane, float(LANES)), axis=1, keepdims=True)
        vals.append(mx)
        idxs.append(idx)
        cur = jnp.where(lane == idx, NEG_BIG, cur)
    ex = [jnp.exp(v - vals[0]) for v in vals]
    den = ex[0] + ex[1] + ex[2] + ex[3]
    gates = [e / den for e in ex]

    @pl.when(i == 0)
    def _zero():
        carry_s[...] = jnp.zeros_like(carry_s)

    sel = [lane == idx for idx in idxs]
    member = jnp.zeros((tm, LANES), F32)
    for s_ in sel:
        member = jnp.where(s_, 1.0, member)
    r_id = lax.broadcasted_iota(I32, (tm, tm), 0)
    c_id = lax.broadcasted_iota(I32, (tm, tm), 1)
    lower = jnp.where(c_id < r_id, 1.0, 0.0).astype(BF16)
    pref = jnp.dot(lower, member.astype(BF16), preferred_element_type=F32) + carry_s[...]
    ranks = [jnp.sum(jnp.where(s_, pref, 0.0), axis=1, keepdims=True) for s_ in sel]
    carry_s[...] = carry_s[...] + jnp.sum(member, axis=0, keepdims=True)
    cnt_ref[...] = carry_s[...].astype(I32)

    e_out = jnp.zeros((tm, LANES), F32)
    g_out = jnp.zeros((tm, LANES), F32)
    r_out = jnp.zeros((tm, LANES), F32)
    for k in range(TOP_K):
        at_k = lane == float(k)
        e_out = jnp.where(at_k, idxs[k], e_out)
        g_out = jnp.where(at_k, gates[k], g_out)
        r_out = jnp.where(at_k, ranks[k], r_out)
    e_ref[...] = e_out.astype(I32)
    gt_ref[...] = g_out
    rk_ref[...] = r_out.astype(I32)


def _mix_kernel(xp_ref, ap_ref, u_ref, up_ref, xs_ref, as_ref, ue_ref, *refs, n_prompt_blocks):
    ext_s = refs[-1]
    refs = refs[:-1]
    i = pl.program_id(0)

    @pl.when(i < n_prompt_blocks)
    def _prompt():
        tm = xp_ref.shape[0]
        ext_s[0:16, :] = jnp.where(i == 0, 0.0, up_ref[...])
        u = u_ref[...]
        ext_s[16:16 + tm, :] = u
        pos = i * tm + lax.broadcasted_iota(I32, (tm, 1), 0)
        means = []
        for g, w in enumerate(POOL_WINDOWS):
            sl = slice(g * POOL_GROUP, (g + 1) * POOL_GROUP)
            s = u[:, sl]
            for j in range(1, w):
                s = s + ext_s[16 - j:16 - j + tm, sl]
            cnt = jnp.minimum(pos + 1, w).astype(F32)
            means.append(s / cnt)
        pooled = jnp.concatenate(means, axis=1)
        _mix_tail(xp_ref[...], ap_ref[...], pooled, u, *refs)

    @pl.when(i >= n_prompt_blocks)
    def _sample():
        bs = ue_ref.shape[0]
        t = ue_ref.shape[1] - 16
        u = ue_ref[:, 16:16 + t, :]
        means = []
        for g, w in enumerate(POOL_WINDOWS):
            sl = slice(g * POOL_GROUP, (g + 1) * POOL_GROUP)
            s = u[:, :, sl]
            for j in range(1, w):
                s = s + ue_ref[:, 16 - j:16 - j + t, sl]
            means.append(s / float(w))
        pooled = jnp.concatenate(means, axis=2).reshape(bs * t, POOL_WIDTH)
        _mix_tail(xs_ref[...], as_ref[...].astype(BF16), pooled, u.reshape(bs * t, POOL_WIDTH),
                  *refs)


def _mix(xp, a_p, u_p, xs, a_s, u_ext, params, *, tm):
    t_p, t_s = xp.shape[0], xs.shape[0]
    t = t_p + t_s
    npb = t_p // tm
    seq_t = u_ext.shape[1] - 16
    bs = tm // seq_t
    row = lambda i: (i, 0)
    const = lambda i: (0, 0)
    const3 = lambda i: (0, 0, 0)
    prow = lambda i: (jnp.minimum(i, npb - 1), 0)
    srow = lambda i: (jnp.maximum(i - npb, 0), 0)
    (wpool16, pscale, wout16, ln_in_g, ln_in_b, ln1_g, ln1_b, wr_pad, br_pad) = params
    param_specs = [pl.BlockSpec((4, POOL_GROUP, POOL_GROUP), const3),
                   pl.BlockSpec((1, POOL_WIDTH), const),
                   pl.BlockSpec((D_MODEL, D_MODEL), const),
                   pl.BlockSpec((1, D_MODEL), const), pl.BlockSpec((1, D_MODEL), const),
                   pl.BlockSpec((1, D_MODEL), const), pl.BlockSpec((1, D_MODEL), const),
                   pl.BlockSpec((D_MODEL, LANES), const), pl.BlockSpec((1, LANES), const)]
    data = [xp, a_p, u_p, u_p, xs, a_s, u_ext]
    data_specs = [pl.BlockSpec((tm, D_MODEL), prow), pl.BlockSpec((tm, 512), prow),
                  pl.BlockSpec((tm, 512), prow),
                  pl.BlockSpec((16, 512),
                               lambda i: (jnp.maximum(jnp.minimum(i, npb - 1) * (tm // 16) - 1, 0), 0)),
                  pl.BlockSpec((tm, D_MODEL), srow), pl.BlockSpec((tm, 512), srow),
                  pl.BlockSpec((bs, 16 + seq_t, 512), lambda i: (jnp.maximum(i - npb, 0), 0, 0))]
    scratch = [pltpu.VMEM((1, LANES), F32), pltpu.VMEM((16 + tm, 512), F32)]
    out_shape = [jax.ShapeDtypeStruct((t, D_MODEL), F32),
                 jax.ShapeDtypeStruct((t * ROW_TILES, LANES), F32),
                 jax.ShapeDtypeStruct((t, LANES), I32),
                 jax.ShapeDtypeStruct((t, LANES), F32),
                 jax.ShapeDtypeStruct((t, LANES), I32),
                 jax.ShapeDtypeStruct((1, LANES), I32)]
    out_specs = [pl.BlockSpec((tm, D_MODEL), row), pl.BlockSpec((tm * ROW_TILES, LANES), row),
                 pl.BlockSpec((tm, LANES), row),
                 pl.BlockSpec((tm, LANES), row), pl.BlockSpec((tm, LANES), row),
                 pl.BlockSpec((1, LANES), const)]
    return pl.pallas_call(
        functools.partial(_mix_kernel, n_prompt_blocks=npb),
        grid=(t // tm,), in_specs=data_specs + param_specs, out_specs=out_specs,
        out_shape=out_shape, scratch_shapes=scratch,
        compiler_params=pltpu.CompilerParams(
            dimension_semantics=("arbitrary",), vmem_limit_bytes=VMEM_LIMIT),
        name="mix",
    )(*data, wpool16, pscale, wout16, ln_in_g, ln_in_b, ln1_g, ln1_b, wr_pad, br_pad)


def _tile_copy(src, src_row, dst, dst_row, sem):
    return pltpu.make_async_copy(
        src.at[pl.ds(pl.multiple_of(src_row * ROW_TILES, ROW_TILES), ROW_TILES)],
        dst.at[pl.ds(pl.multiple_of(dst_row * ROW_TILES, ROW_TILES), ROW_TILES)], sem)


def _experts_kernel(be_ref, nu_ref, tok0_ref, tokn_ref, dst_ref, x_hbm, wgu_ref, bgu_ref,
                    wd_ref, bd_ref, yc_hbm, wgu16_s, wd16_s, xbuf, ybuf, gsem, ssem, *, tm):
    i = pl.program_id(0)
    nu = nu_ref[0]
    slot = i % 2
    e = be_ref[i]
    prev = be_ref[jnp.maximum(i - 1, 0)]
    half = tm // 2

    def gather_start(tok_ref, sl):
        for r in range(tm):
            _tile_copy(x_hbm, tok_ref[0, 0, r], xbuf.at[sl], r, gsem.at[sl]).start(priority=r % 2)

    def gather_wait(sl):
        for r in range(tm):
            _tile_copy(x_hbm, 0, xbuf.at[sl], r, gsem.at[sl]).wait()

    def scatter(sl, wait):
        for r in range(tm):
            cp = _tile_copy(ybuf.at[sl], r, yc_hbm, dst_ref[0, 0, r], ssem)
            cp.wait() if wait else cp.start(priority=r % 2)

    @pl.when(i == 0)
    def _prime():
        gather_start(tok0_ref, 0)
        ybuf[1] = jnp.zeros(ybuf.shape[1:], F32)

    @pl.when((i == 0) | (e != prev))
    def _cast():
        wgu16_s[...] = wgu_ref[0].astype(BF16)
        wd16_s[...] = wd_ref[0].astype(BF16)

    @pl.when(i < nu)
    def _run():
        gather_wait(slot)
        gus = []
        for hh in range(2):
            x = _from_row_tiles(xbuf.at[slot], half, hh * half).astype(BF16)
            gus.append(jnp.dot(x, wgu16_s[...], preferred_element_type=F32) + bgu_ref[0])
            if hh == 0:
                gather_start(tokn_ref, 1 - slot)
                scatter(1 - slot, wait=False)
        for hh in range(2):
            gate = jnp.minimum(gus[hh][:, 0:D_FF], SWIGLU_LIMIT)
            up = jnp.clip(gus[hh][:, D_FF:2 * D_FF], -SWIGLU_LIMIT, SWIGLU_LIMIT)
            hdn = (up + 1.0) * (gate * jax.nn.sigmoid(SWIGLU_ALPHA * gate))
            y = jnp.dot(hdn.astype(BF16), wd16_s[...], preferred_element_type=F32) + bd_ref[0]
            for c in range(ROW_TILES):
                ybuf[slot, pl.ds(hh * half * ROW_TILES + c, half, stride=ROW_TILES), :] = (
                    y[:, c * LANES:(c + 1) * LANES])
        scatter(1 - slot, wait=True)

    @pl.when(i == nu)
    def _flush():
        gather_wait(slot)
        scatter(1 - slot, wait=False)
        scatter(1 - slot, wait=True)


def _moe_experts(h1t, row_tok, row_dst_prev, block_e, n_used, w_gu, b_gu, w_down, b_down,
                 *, tm, n_out_rows):
    rows = row_tok.shape[0]
    nblk = rows // tm
    tok3 = row_tok.reshape(nblk, 1, tm)
    smem_blk = lambda f: pl.BlockSpec((1, 1, tm), f, memory_space=pltpu.SMEM)
    grid_spec = pltpu.PrefetchScalarGridSpec(
        num_scalar_prefetch=2, grid=(nblk,),
        in_specs=[smem_blk(lambda i, be, nu: (0, 0, 0)),
                  smem_blk(lambda i, be, nu: (jnp.minimum(i + 1, nblk - 1), 0, 0)),
                  smem_blk(lambda i, be, nu: (i, 0, 0)),
                  pl.BlockSpec(memory_space=pl.ANY),
                  pl.BlockSpec((1, D_MODEL, 2 * D_FF), lambda i, be, nu: (be[i], 0, 0)),
                  pl.BlockSpec((1, 1, 2 * D_FF), lambda i, be, nu: (be[i], 0, 0)),
                  pl.BlockSpec((1, D_FF, D_MODEL), lambda i, be, nu: (be[i], 0, 0)),
                  pl.BlockSpec((1, 1, D_MODEL), lambda i, be, nu: (be[i], 0, 0))],
        out_specs=pl.BlockSpec(memory_space=pl.ANY),
        scratch_shapes=[pltpu.VMEM((D_MODEL, 2 * D_FF), BF16),
                        pltpu.VMEM((D_FF, D_MODEL), BF16),
                        pltpu.VMEM((2, tm * ROW_TILES, LANES), F32),
                        pltpu.VMEM((2, tm * ROW_TILES, LANES), F32),
                        pltpu.SemaphoreType.DMA((2,)),
                        pltpu.SemaphoreType.DMA])
    return pl.pallas_call(
        functools.partial(_experts_kernel, tm=tm), grid_spec=grid_spec,
        out_shape=jax.ShapeDtypeStruct((n_out_rows * ROW_TILES, LANES), F32),
        compiler_params=pltpu.CompilerParams(
            dimension_semantics=("arbitrary",), vmem_limit_bytes=VMEM_LIMIT),
        name="moe_experts",
    )(block_e, n_used, tok3, tok3, row_dst_prev.reshape(nblk, 1, tm), h1t,
      w_gu, b_gu.reshape(N_EXPERTS, 1, 2 * D_FF), w_down, b_down.reshape(N_EXPERTS, 1, D_MODEL))


def _finish_kernel(y0_ref, y1_ref, y2_ref, y3_ref, h1_ref, gt_ref, g2_ref, b2_ref,
                   oa_ref, ob_ref, *, tm, nblk_a):
    i = pl.program_id(0)
    gt = gt_ref[...]
    f = _from_row_tiles(y0_ref, tm) * gt[:, 0:1]
    for k, y_ref in enumerate((y1_ref, y2_ref, y3_ref), start=1):
        f = f + _from_row_tiles(y_ref, tm) * gt[:, k:k + 1]
    y = _ln(DN_ALPHA * h1_ref[...] + f, g2_ref[...], b2_ref[...])

    @pl.when(i < nblk_a)
    def _first():
        oa_ref[...] = y

    @pl.when(i >= nblk_a)
    def _second():
        ob_ref[...] = y


def _moe_finish(yc, h1, gates, ln2_g, ln2_b, *, tm, t_first):
    t = h1.shape[0]
    nblk = t // tm
    nblk_a = t_first // tm
    row = lambda i: (i, 0)
    const = lambda i: (0, 0)
    slot_spec = lambda k: pl.BlockSpec((tm * ROW_TILES, LANES), lambda i: (k * nblk + i, 0))
    return pl.pallas_call(
        functools.partial(_finish_kernel, tm=tm, nblk_a=nblk_a), grid=(nblk,),
        in_specs=[slot_spec(0), slot_spec(1), slot_spec(2), slot_spec(3),
                  pl.BlockSpec((tm, D_MODEL), row),
                  pl.BlockSpec((tm, LANES), row),
                  pl.BlockSpec((1, D_MODEL), const), pl.BlockSpec((1, D_MODEL), const)],
        out_specs=[pl.BlockSpec((tm, D_MODEL), lambda i: (jnp.minimum(i, nblk_a - 1), 0)),
                   pl.BlockSpec((tm, D_MODEL), lambda i: (jnp.maximum(i - nblk_a, 0), 0))],
        out_shape=[jax.ShapeDtypeStruct((t_first, D_MODEL), F32),
                   jax.ShapeDtypeStruct((t - t_first, D_MODEL), F32)],
        compiler_params=pltpu.CompilerParams(
            dimension_semantics=("arbitrary",), vmem_limit_bytes=VMEM_LIMIT),
        name="moe_finish",
    )(yc, yc, yc, yc, h1, gates, ln2_g, ln2_b)


def _moe(h1, h1t, top_e, gates, rank, counts, w_gu, b_gu, w_down, b_down, ln2_g, ln2_b,
         *, tm_e, tm_c, t_first):
    t = h1.shape[0]
    counts = counts[0, :N_EXPERTS]
    padded = (counts + tm_e - 1) // tm_e * tm_e
    pad_end = jnp.cumsum(padded)
    pad_start = pad_end - padded
    e4 = top_e[:, :TOP_K]
    onehot = e4[:, :, None] == jnp.arange(N_EXPERTS, dtype=I32)[None, None, :]
    dest = rank[:, :TOP_K] + jnp.sum(jnp.where(onehot, pad_start[None, None, :], 0), axis=-1)
    nblk = (t * TOP_K) // tm_e + N_EXPERTS
    rows = nblk * tm_e
    starts = jnp.arange(nblk, dtype=I32) * tm_e
    block_e = jnp.minimum(jnp.sum((pad_end[None, :] <= starts[:, None]).astype(I32), axis=1),
                          N_EXPERTS - 1).astype(I32)
    nu_e = (pad_end[-1] // tm_e).astype(I32).reshape(1)
    flat = jnp.arange(t * TOP_K, dtype=I32)
    code = (flat % TOP_K) * t + flat // TOP_K
    inv = jnp.full((rows,), -1, I32).at[dest.reshape(-1)].set(code, unique_indices=True)
    spare = TOP_K * t + jnp.arange(rows, dtype=I32) % tm_e
    row_tok = jnp.where(inv < 0, 0, inv % t)
    row_dst = jnp.where(inv < 0, spare, inv)
    row_dst_prev = jnp.concatenate([spare[:tm_e], row_dst[:rows - tm_e]])
    yc = _moe_experts(h1t, row_tok, row_dst_prev, block_e, nu_e, w_gu, b_gu, w_down, b_down,
                      tm=tm_e, n_out_rows=TOP_K * t + tm_e)
    return _moe_finish(yc, h1, gates, ln2_g, ln2_b, tm=tm_c, t_first=t_first)


def kernel(x_prompt, x_sample, cache_k, cache_v, state_pool, page_table, ln_in_g, ln_in_b, w_in, lambda_q1, lambda_k1, lambda_q2, lambda_k2, subln_g, rel_bias_table, w_pool, pool_scale, w_out, ln1_g, ln1_b, w_router, b_router, w_gate_up, b_gate_up, w_down, b_down, ln2_g, ln2_b):
    l = 0
    bsz, seq, _ = x_prompt.shape
    nb, dec_t, _ = x_sample.shape
    assert bsz == 1
    r2 = lambda a: a.reshape(1, -1)
    w_in16 = w_in[l].astype(BF16)
    lam_params = jnp.stack([lambda_q1[l], lambda_k1[l], lambda_q2[l], lambda_k2[l]]).astype(F32)
    wr_pad = jnp.pad(w_router[l], ((0, 0), (0, LANES - N_EXPERTS)))
    br_pad = jnp.pad(b_router[l], (0, LANES - N_EXPERTS)).reshape(1, LANES)
    mix_params = (w_pool[l].astype(BF16), r2(pool_scale[l]), w_out[l].astype(BF16),
                  r2(ln_in_g), r2(ln_in_b), r2(ln1_g[l]), r2(ln1_b[l]), wr_pad, br_pad)

    xp = x_prompt.reshape(seq, D_MODEL)
    xs = x_sample.reshape(nb * dec_t, D_MODEL)
    k_p, v_p, u_p, qt_p, kb_p, vt_p = _inproj(xp, r2(ln_in_g), r2(ln_in_b), w_in16,
                                              tm=512, prompt=True)
    q_s, k_s, v_s, u_s = _inproj(xs, r2(ln_in_g), r2(ln_in_b), w_in16, tm=512, prompt=False)
    a_p, a_s = _attention(qt_p, kb_p, vt_p, q_s, k_s, v_s, cache_k[l], cache_v[l], page_table,
                          rel_bias_table, lam_params, subln_g[l], tq=512, pg=16)

    u_ext = jnp.concatenate([jnp.zeros((nb, 1, POOL_WIDTH), F32), state_pool[l],
                             u_s.reshape(nb, dec_t, POOL_WIDTH)], axis=1)
    h1, h1t, top_e, gates, rank, cnt = _mix(xp, a_p, u_p, xs, a_s, u_ext, mix_params, tm=512)
    y_p, y_s = _moe(h1, h1t, top_e, gates, rank, cnt, w_gate_up[l], b_gate_up[l], w_down[l],
                    b_down[l], r2(ln2_g[l]), r2(ln2_b[l]), tm_e=256, tm_c=256, t_first=seq)

    k_prompt = k_p.reshape(1, bsz, seq, N_HEADS, 2, QK_DIM)
    v_prompt = v_p.reshape(1, bsz, seq, N_HEADS, V_DIM)
    pool_prompt = u_p[seq - POOL_STATE:].reshape(1, bsz, POOL_STATE, POOL_WIDTH)
    k_sample = k_s.reshape(1, nb, dec_t, N_HEADS, 2, QK_DIM)
    v_sample = v_s.reshape(1, nb, dec_t, N_HEADS, V_DIM)
    pool_sample = u_ext[:, 16 + dec_t - POOL_STATE:].reshape(1, nb, POOL_STATE, POOL_WIDTH)
    return (y_p.reshape(bsz, seq, D_MODEL), y_s.reshape(nb, dec_t, D_MODEL),
            k_prompt, v_prompt, pool_prompt, k_sample, v_sample, pool_sample)
```
